```python
import math
import jax
import jax.numpy as jnp
from jax import lax
import numpy as np

D_MODEL = 1024
BATCH = 2
SEQ = 8192
DEPTH = 2
DEC_BATCH = 128
DEC_SEQ = 4
PAST_LEN = 16384
PAGE_SIZE = 128

F32 = jnp.float32
EPS = 1e-6
N_BRANCH = 4
BRANCH_W = D_MODEL // 4
RW_HEADS = 4
RW_HEAD = BRANCH_W // RW_HEADS
RW_DECAY_LORA = 32
RW_AAA_LORA = 32
RW_GATE_LORA = 64
RW_SHIFT_W = 3 * BRANCH_W + RW_DECAY_LORA + RW_AAA_LORA + RW_GATE_LORA
RW_LN_EPS = 64e-5
MLA_HEADS = 4
MLA_NOPE = 64
MLA_ROPE = 32
MLA_QK = MLA_NOPE + MLA_ROPE
MLA_V = BRANCH_W // MLA_HEADS
MLA_Q_LORA = 256
MLA_KV_LORA = 128
ROPE_BASE = 10000.0
GDN_HEADS = 4
GDN_DK = 64
GDN_DV = BRANCH_W // GDN_HEADS
GDN_CONV = 4
GDN_CHUNK = 64
GDN_QKV_W = GDN_HEADS * (2 * GDN_DK + GDN_DV)
DIFF_HEADS = 4
DIFF_D = 32
DIFF_V = 2 * DIFF_D
DIFF_QKV_W = DIFF_HEADS * (4 * DIFF_D + DIFF_V)
REL_BUCKETS = 32
REL_MAX_DIST = 128
Q_BLOCK = 128
D_FF = 2816
N_EXPERTS = 8
TOP_K = 2
D_FF_EXPERT = 1792
PROJ_SIZES = (N_BRANCH * D_MODEL, RW_SHIFT_W, MLA_Q_LORA, MLA_KV_LORA, MLA_ROPE,
              GDN_QKV_W, GDN_HEADS, GDN_HEADS, GDN_HEADS * GDN_DV, DIFF_QKV_W)
PROJ_W = sum(PROJ_SIZES)
STATE_KEYS = ('mla_ckv', 'mla_krope', 'diff_k', 'diff_v', 'rwkv', 'rwkv_shift', 'gdn', 'gdn_conv')

kernel_name = 'hybrid_gated_branches_decode_step'


def _split_cols(z, sizes):
    out, o = [], 0
    for w in sizes:
        out.append(z[..., o:o + w])
        o += w
    return out


def _rmsnorm(x, g):
    xf = x.astype(F32)
    return (xf * lax.rsqrt(jnp.mean(xf * xf, axis=-1, keepdims=True) + EPS) * g).astype(x.dtype)


def _l2norm(x):
    xf = x.astype(F32)
    return xf * lax.rsqrt(jnp.maximum(jnp.sum(xf * xf, axis=-1, keepdims=True), 1e-12))


def _rope(x, pos):
    half = x.shape[-1] // 2
    inv = ROPE_BASE ** (-jnp.arange(half, dtype=F32) / half)
    ang = pos.astype(F32)[:, None] * inv[None, :]
    cos = jnp.cos(ang)[None, :, None, :]
    sin = jnp.sin(ang)[None, :, None, :]
    x1, x2 = x[..., :half], x[..., half:]
    return jnp.concatenate([x1 * cos - x2 * sin, x1 * sin + x2 * cos], axis=-1).astype(x.dtype)


def _rel_bias(rel_table, qpos, kpos):
    n = jnp.maximum(qpos[:, None] - kpos[None, :], 0)
    exact = REL_BUCKETS // 2
    nf = jnp.maximum(n, 1).astype(F32)
    large = exact + (jnp.log(nf / exact) / math.log(REL_MAX_DIST / exact)
                     * (REL_BUCKETS - exact)).astype(jnp.int32)
    bucket = jnp.where(n < exact, n, jnp.minimum(large, REL_BUCKETS - 1))
    return jnp.moveaxis(rel_table[bucket], -1, 0).astype(F32)


def _softmax_map(q, k, valid, bias=None):
    s = jnp.einsum('...qhd,...khd->...hqk', q, k).astype(F32) * (q.shape[-1] ** -0.5)
    if bias is not None:
        s = s + bias
    return jax.nn.softmax(jnp.where(valid, s, -jnp.inf), axis=-1)


def _diff_map(q, k, valid, bias, lam):
    p1 = _softmax_map(q[..., :DIFF_D], k[..., :DIFF_D], valid, bias)
    p2 = _softmax_map(q[..., DIFF_D:], k[..., DIFF_D:], valid, bias)
    return p1 - lam * p2


def _prompt_sweep(block_fn, *qs):
    B, S = qs[0].shape[:2]
    nb = S // Q_BLOCK
    blocks = tuple(jnp.moveaxis(a.reshape((B, nb, Q_BLOCK) + a.shape[2:]), 1, 0) for a in qs)

    def run(args):
        i, qb = args
        return block_fn(i * Q_BLOCK + jnp.arange(Q_BLOCK), *qb)

    out = lax.map(run, (jnp.arange(nb), blocks))
    return jnp.moveaxis(out, 0, 1).reshape((B, S) + out.shape[3:])


def _rwkv7(zr, shift_prev, s0, lp):
    B, L, _ = zr.shape
    prev = jnp.concatenate([shift_prev[:, None].astype(zr.dtype), zr[:, :-1]], axis=1)
    zs = (zr + (prev - zr) * lp['rw_mu']).astype(F32)
    r, k, v, wd, ad, gd = _split_cols(zs, (BRANCH_W, BRANCH_W, BRANCH_W,
                                           RW_DECAY_LORA, RW_AAA_LORA, RW_GATE_LORA))
    w_log = -jax.nn.softplus(-(lp['rw_w0'] + jnp.tanh(wd) @ lp['rw_w_up'])) - 0.5
    decay = jnp.exp(-jnp.exp(w_log))
    a = jax.nn.sigmoid(lp['rw_a0'] + ad @ lp['rw_a_up'])
    g = jax.nn.sigmoid(gd) @ lp['rw_g_up']
    heads = lambda t: t.reshape(B, L, RW_HEADS, RW_HEAD)
    r, k, v, decay, a = heads(r), heads(k), heads(v), heads(decay), heads(a)
    kk = _l2norm(k * lp['rw_k_k'].reshape(RW_HEADS, RW_HEAD))
    k = k * (1.0 + (a - 1.0) * lp['rw_k_a'].reshape(RW_HEADS, RW_HEAD))

    def step(S, xs):
        r_t, k_t, v_t, w_t, kk_t, a_t = xs
        s_kk = jnp.einsum('bhvk,bhk->bhv', S, kk_t)
        S = (S * w_t[:, :, None, :] - s_kk[..., None] * (kk_t * a_t)[:, :, None, :]
             + v_t[..., None] * k_t[:, :, None, :])
        return S, jnp.einsum('bhvk,bhk->bhv', S, r_t)

    xs = tuple(jnp.moveaxis(t, 1, 0) for t in (r, k, v, decay, kk, a))
    s_new, y = lax.scan(step, s0.astype(F32), xs)
    y = jnp.moveaxis(y, 0, 1)
    mu = jnp.mean(y, axis=-1, keepdims=True)
    var = jnp.mean(jnp.square(y - mu), axis=-1, keepdims=True)
    y = ((y - mu) * lax.rsqrt(var + RW_LN_EPS) * lp['rw_ln_g'].reshape(RW_HEADS, RW_HEAD)
         + lp['rw_ln_b'].reshape(RW_HEADS, RW_HEAD))
    y = y + jnp.sum(r * k * lp['rw_r_k'], axis=-1, keepdims=True) * v
    return y.reshape(B, L, BRANCH_W) * g, s_new, zr[:, -1]


def _gated_delta(q, k, v, beta, g, s0):
    B, L, H, DK = q.shape
    DV = v.shape[-1]
    C = math.gcd(L, GDN_CHUNK)
    N = L // C

    def chunks(t):
        t = jnp.moveaxis(t, 2, 1)
        return t.reshape((B, H, N, C) + t.shape[3:])

    q, k, v, beta, g = chunks(q * DK ** -0.5), chunks(k), chunks(v), chunks(beta), chunks(g)
    gc = jnp.cumsum(g, axis=-1)
    idx = jnp.arange(C)
    causal = idx[:, None] >= idx[None, :]
    strict = idx[:, None] > idx[None, :]
    gdiff = gc[..., :, None] - gc[..., None, :]
    decay = jnp.where(causal, jnp.exp(jnp.where(causal, gdiff, 0.0)), 0.0)
    kb = k * beta[..., None]
    a_mat = jnp.eye(C, dtype=F32) + jnp.where(strict, jnp.einsum('bhnid,bhnjd->bhnij', kb, k) * decay, 0.0)
    u = lax.linalg.triangular_solve(a_mat, v * beta[..., None], left_side=True, lower=True,
                                    unit_diagonal=True)
    w = lax.linalg.triangular_solve(a_mat, kb * jnp.exp(gc)[..., None], left_side=True, lower=True,
                                    unit_diagonal=True)
    qk = jnp.einsum('bhnid,bhnjd->bhnij', q, k) * decay
    qg = q * jnp.exp(gc)[..., None]
    kg = k * jnp.exp(gc[..., -1:] - gc)[..., None]
    g_last = jnp.exp(gc[..., -1])

    def step(S, xs):
        qk_n, qg_n, kg_n, u_n, w_n, gl_n = xs
        v_new = u_n - jnp.einsum('bhcd,bhde->bhce', w_n, S)
        o = jnp.einsum('bhcd,bhde->bhce', qg_n, S) + jnp.einsum('bhij,bhje->bhie', qk_n, v_new)
        S = S * gl_n[..., None, None] + jnp.einsum('bhcd,bhce->bhde', kg_n, v_new)
        return S, o

    xs = tuple(jnp.moveaxis(t, 2, 0) for t in (qk, qg, kg, u, w, g_last))
    s_new, o = lax.scan(step, s0.astype(F32), xs)
    o = jnp.moveaxis(o, 0, 2).reshape(B, H, L, DV)
    return jnp.moveaxis(o, 1, 2), s_new


def _gdn(zqkv, zb, za, zz, conv_prev, s0, lp):
    B, L, _ = zqkv.shape
    xc = jnp.concatenate([conv_prev.astype(zqkv.dtype), zqkv], axis=1)
    wc = lp['gdn_conv_w']
    conv = sum(xc[:, j:j + L] * wc[j] for j in range(GDN_CONV))
    qkv = jax.nn.silu(conv.astype(F32))
    q, k, v = _split_cols(qkv, (GDN_HEADS * GDN_DK, GDN_HEADS * GDN_DK, GDN_HEADS * GDN_DV))
    q = _l2norm(q.reshape(B, L, GDN_HEADS, GDN_DK))
    k = _l2norm(k.reshape(B, L, GDN_HEADS, GDN_DK))
    v = v.reshape(B, L, GDN_HEADS, GDN_DV)
    beta = jax.nn.sigmoid(zb.astype(F32))
    g = -jnp.exp(lp['gdn_a_log'].astype(F32)) * jax.nn.softplus(za.astype(F32) + lp['gdn_dt_bias'])
    o, s_new = _gated_delta(q, k, v, beta, g, s0)
    o = _rmsnorm(o, lp['gdn_norm_g']) * jax.nn.silu(zz.astype(F32).reshape(B, L, GDN_HEADS, GDN_DV))
    return o.reshape(B, L, BRANCH_W), s_new, xc[:, -(GDN_CONV - 1):]


def _mla_project(zq, zkv, zkr, pos, lp):
    cq = _rmsnorm(zq, lp['mla_q_norm_g'])
    q = jnp.einsum('blr,rhd->blhd', cq, lp['mla_w_uq'])
    q = jnp.concatenate([q[..., :MLA_NOPE], _rope(q[..., MLA_NOPE:], pos)], axis=-1)
    q = _rmsnorm(q, lp['mla_qn_g'])
    ckv = _rmsnorm(zkv, lp['mla_kv_norm_g'])
    kr = _rope(zkr[:, :, None, :], pos)[:, :, 0]
    return q, ckv, kr


def _mla_keys(ckv, kr, lp):
    k_nope = jnp.einsum('...lr,rhd->...lhd', ckv, lp['mla_w_uk'])
    v = jnp.einsum('...lr,rhd->...lhd', ckv, lp['mla_w_uv'])
    k_rope = jnp.broadcast_to(kr[..., None, :], k_nope.shape[:-1] + (MLA_ROPE,))
    k = _rmsnorm(jnp.concatenate([k_nope, k_rope], axis=-1), lp['mla_kn_g'])
    return k, v


def _diff_project(zd, lp):
    B, L, _ = zd.shape
    zq, zk, zv = _split_cols(zd, (DIFF_HEADS * 2 * DIFF_D, DIFF_HEADS * 2 * DIFF_D, DIFF_HEADS * DIFF_V))
    q = _rmsnorm(zq.reshape(B, L, DIFF_HEADS, 2, DIFF_D), lp['dif_qn_g'])
    k = _rmsnorm(zk.reshape(B, L, DIFF_HEADS, 2, DIFF_D), lp['dif_kn_g'])
    return (q.reshape(B, L, DIFF_HEADS, 2 * DIFF_D), k.reshape(B, L, DIFF_HEADS, 2 * DIFF_D),
            zv.reshape(B, L, DIFF_HEADS, DIFF_V))


def _diff_lambda(lam, l):
    lam_init = 0.8 - 0.6 * math.exp(-0.3 * l)
    lf = lam.astype(F32)
    return jnp.exp(jnp.sum(lf[0] * lf[1])) - jnp.exp(jnp.sum(lf[2] * lf[3])) + lam_init, lam_init


def _swiglu(h, wg, wu, wd):
    return (jax.nn.silu(h @ wg) * (h @ wu)) @ wd


def _moe(h, router, wg, wu, wd):
    logits = (h @ router).astype(F32)
    top_v, top_i = lax.top_k(logits, TOP_K)
    top_w = jax.nn.softmax(top_v, axis=-1)
    combine = jnp.sum(jax.nn.one_hot(top_i, N_EXPERTS, dtype=F32) * top_w[..., None], axis=-2)
    return sum(combine[..., e:e + 1] * _swiglu(h, wg[e], wu[e], wd[e]) for e in range(N_EXPERTS))


def _mixer_layer(x, pos, l, lp, rel_table, st, paged):
    B, L, _ = x.shape
    h = _rmsnorm(x, lp['norm1_g'])
    z = h @ lp['w_in']
    zg, zrw, zmq, zmkv, zmkr, zqkv, zb, za, zz, zdif = _split_cols(z, PROJ_SIZES)
    gates = jax.nn.sigmoid(zg.astype(F32)).reshape(B, L, N_BRANCH, D_MODEL)

    o_rw, rw_s, rw_shift = _rwkv7(zrw, st['rwkv_shift'], st['rwkv'], lp)
    o_gdn, gdn_s, gdn_conv = _gdn(zqkv, zb, za, zz, st['gdn_conv'], st['gdn'], lp)
    q_m, ckv, kr = _mla_project(zmq, zmkv, zmkr, pos, lp)
    q_d, k_d, v_d = _diff_project(zdif, lp)
    lam, lam_init = _diff_lambda(lp['dif_lam'], l)

    if paged is None:
        kpos = pos
        k_m, v_m = _mla_keys(ckv, kr, lp)

        def mla_block(qpos, qb):
            p = _softmax_map(qb, k_m, kpos[None, :] <= qpos[:, None])
            return jnp.einsum('...hqk,...khd->...qhd', p, v_m)

        def diff_block(qpos, qb):
            p = _diff_map(qb, k_d, kpos[None, :] <= qpos[:, None], _rel_bias(rel_table, qpos, kpos), lam)
            return jnp.einsum('...hqk,...khd->...qhd', p, v_d)

        o_mla = _prompt_sweep(mla_block, q_m)
        o_dif = _prompt_sweep(diff_block, q_d)
    else:
        page_table, ckv_pool, kr_pool, dk_pool, dv_pool = paged
        kpos = jnp.arange(page_table.shape[1] * PAGE_SIZE + L)
        valid = kpos[None, :] <= pos[:, None]
        bias = _rel_bias(rel_table, pos, kpos)

        def mla_seq(args):
            pages, qi, ci, ri = args
            c = jnp.concatenate([ckv_pool[l, pages].reshape(-1, MLA_KV_LORA).astype(ci.dtype), ci], axis=0)
            r = jnp.concatenate([kr_pool[l, pages].reshape(-1, MLA_ROPE).astype(ri.dtype), ri], axis=0)
            k, v = _mla_keys(c, r, lp)
            return jnp.einsum('...hqk,...khd->...qhd', _softmax_map(qi, k, valid), v)

        def diff_seq(args):
            pages, qi, ki, vi = args
            kc = jnp.concatenate([dk_pool[l, pages].reshape((-1,) + ki.shape[1:]).astype(ki.dtype), ki], axis=0)
            vc = jnp.concatenate([dv_pool[l, pages].reshape((-1,) + vi.shape[1:]).astype(vi.dtype), vi], axis=0)
            return jnp.einsum('...hqk,...khd->...qhd', _diff_map(qi, kc, valid, bias, lam), vc)

        o_mla = lax.map(mla_seq, (page_table, q_m, ckv, kr))
        o_dif = lax.map(diff_seq, (page_table, q_d, k_d, v_d))

    o_dif = _rmsnorm(o_dif, lp['dif_subln_g']) * (1.0 - lam_init)
    branches = jnp.stack([o_rw.astype(F32), o_mla.reshape(B, L, BRANCH_W).astype(F32),
                          o_gdn.astype(F32), o_dif.reshape(B, L, BRANCH_W).astype(F32)], axis=2)
    merged = jnp.sum(gates * jnp.einsum('blnc,ncd->blnd', branches, lp['w_branch']), axis=2)
    x = x + (merged @ lp['w_out']).astype(x.dtype)
    new = dict(mla_ckv=ckv, mla_krope=kr, diff_k=k_d, diff_v=v_d,
               rwkv=rw_s, rwkv_shift=rw_shift, gdn=gdn_s, gdn_conv=gdn_conv)
    return x, new


def setup_inputs(seed: int = 0) -> dict:
    key = jax.random.key(seed)
    ks = jax.random.split(key, 80)
    ctr = [0]

    def nk():
        ctr[0] += 1
        return ks[ctr[0] - 1]

    def nrm(shape, scale=1.0):
        return scale * jax.random.normal(nk(), shape, F32)

    def gain(shape):
        return 1.0 + 0.05 * jax.random.normal(nk(), shape, F32)

    def unif(shape, lo, hi):
        return jax.random.uniform(nk(), shape, F32, lo, hi)

    n_pages = PAST_LEN // PAGE_SIZE
    n_used = DEC_BATCH * n_pages
    n_pool = n_used + max(1, n_used // 4)
    n_dense = (DEPTH + 1) // 2
    n_moe = DEPTH // 2
    page_table = jax.random.permutation(nk(), n_pool)[:n_used].reshape(DEC_BATCH, n_pages).astype(jnp.int32)
    dt = jnp.exp(unif((DEPTH, GDN_HEADS), math.log(1e-3), math.log(1e-1)))
    return {
        'x_prompt': nrm((BATCH, SEQ, D_MODEL)),
        'x_sample': nrm((DEC_BATCH, DEC_SEQ, D_MODEL)),
        'cache_mla_ckv': nrm((DEPTH, n_pool, PAGE_SIZE, MLA_KV_LORA)),
        'cache_mla_krope': nrm((DEPTH, n_pool, PAGE_SIZE, MLA_ROPE)),
        'cache_diff_k': nrm((DEPTH, n_pool, PAGE_SIZE, DIFF_HEADS, 2 * DIFF_D)),
        'cache_diff_v': nrm((DEPTH, n_pool, PAGE_SIZE, DIFF_HEADS, DIFF_V)),
        'state_rwkv': nrm((DEPTH, DEC_BATCH, RW_HEADS, RW_HEAD, RW_HEAD), 0.1),
        'state_rwkv_shift': nrm((DEPTH, DEC_BATCH, RW_SHIFT_W)),
        'state_gdn': nrm((DEPTH, DEC_BATCH, GDN_HEADS, GDN_DK, GDN_DV), 0.1),
        'state_gdn_conv': nrm((DEPTH, DEC_BATCH, GDN_CONV - 1, GDN_QKV_W)),
        'page_table': page_table,
        'rel_bias': nrm((REL_BUCKETS, DIFF_HEADS), 0.5),
        'norm1_g': gain((DEPTH, D_MODEL)),
        'norm2_g': gain((DEPTH, D_MODEL)),
        'w_in': nrm((DEPTH, D_MODEL, PROJ_W), D_MODEL ** -0.5),
        'rw_mu': unif((DEPTH, RW_SHIFT_W), 0.0, 1.0),
        'rw_w0': unif((DEPTH, BRANCH_W), -6.0, -1.0),
        'rw_w_up': nrm((DEPTH, RW_DECAY_LORA, BRANCH_W), 0.1),
        'rw_a0': nrm((DEPTH, BRANCH_W), 0.1),
        'rw_a_up': nrm((DEPTH, RW_AAA_LORA, BRANCH_W), 0.1),
        'rw_g_up': nrm((DEPTH, RW_GATE_LORA, BRANCH_W), RW_GATE_LORA ** -0.5),
        'rw_k_k': 0.85 + nrm((DEPTH, BRANCH_W), 0.05),
        'rw_k_a': gain((DEPTH, BRANCH_W)),
        'rw_r_k': nrm((DEPTH, RW_HEADS, RW_HEAD), 0.1),
        'rw_ln_g': gain((DEPTH, BRANCH_W)),
        'rw_ln_b': nrm((DEPTH, BRANCH_W), 0.02),
        'mla_q_norm_g': gain((DEPTH, MLA_Q_LORA)),
        'mla_w_uq': nrm((DEPTH, MLA_Q_LORA, MLA_HEADS, MLA_QK), MLA_Q_LORA ** -0.5),
        'mla_kv_norm_g': gain((DEPTH, MLA_KV_LORA)),
        'mla_w_uk': nrm((DEPTH, MLA_KV_LORA, MLA_HEADS, MLA_NOPE), MLA_KV_LORA ** -0.5),
        'mla_w_uv': nrm((DEPTH, MLA_KV_LORA, MLA_HEADS, MLA_V), MLA_KV_LORA ** -0.5),
        'mla_qn_g': gain((DEPTH, MLA_QK)),
        'mla_kn_g': gain((DEPTH, MLA_QK)),
        'gdn_conv_w': nrm((DEPTH, GDN_CONV, GDN_QKV_W), GDN_CONV ** -0.5),
        'gdn_a_log': jnp.log(unif((DEPTH, GDN_HEADS), 1.0, 16.0)),
        'gdn_dt_bias': dt + jnp.log(-jnp.expm1(-dt)),
        'gdn_norm_g': gain((DEPTH, GDN_DV)),
        'dif_qn_g': gain((DEPTH, 2, DIFF_D)),
        'dif_kn_g': gain((DEPTH, 2, DIFF_D)),
        'dif_lam': nrm((DEPTH, 4, DIFF_D), 0.1),
        'dif_subln_g': gain((DEPTH, DIFF_V)),
        'w_branch': nrm((DEPTH, N_BRANCH, BRANCH_W, D_MODEL), BRANCH_W ** -0.5),
        'w_out': nrm((DEPTH, D_MODEL, D_MODEL), D_MODEL ** -0.5),
        'ffn_w_gate': nrm((n_dense, D_MODEL, D_FF), D_MODEL ** -0.5),
        'ffn_w_up': nrm((n_dense, D_MODEL, D_FF), D_MODEL ** -0.5),
        'ffn_w_down': nrm((n_dense, D_FF, D_MODEL), D_FF ** -0.5),
        'moe_router': nrm((n_moe, D_MODEL, N_EXPERTS), D_MODEL ** -0.5),
        'moe_w_gate': nrm((n_moe, N_EXPERTS, D_MODEL, D_FF_EXPERT), D_MODEL ** -0.5),
        'moe_w_up': nrm((n_moe, N_EXPERTS, D_MODEL, D_FF_EXPERT), D_MODEL ** -0.5),
        'moe_w_down': nrm((n_moe, N_EXPERTS, D_FF_EXPERT, D_MODEL), D_FF_EXPERT ** -0.5),
    }


def reference(x_prompt, x_sample, cache_mla_ckv, cache_mla_krope, cache_diff_k, cache_diff_v,
              state_rwkv, state_rwkv_shift, state_gdn, state_gdn_conv, page_table, rel_bias,
              norm1_g, norm2_g, w_in, rw_mu, rw_w0, rw_w_up, rw_a0, rw_a_up, rw_g_up, rw_k_k,
              rw_k_a, rw_r_k, rw_ln_g, rw_ln_b, mla_q_norm_g, mla_w_uq, mla_kv_norm_g, mla_w_uk,
              mla_w_uv, mla_qn_g, mla_kn_g, gdn_conv_w, gdn_a_log, gdn_dt_bias, gdn_norm_g,
              dif_qn_g, dif_kn_g, dif_lam, dif_subln_g, w_branch, w_out, ffn_w_gate, ffn_w_up,
              ffn_w_down, moe_router, moe_w_gate, moe_w_up, moe_w_down):
    layer_params = dict(
        norm1_g=norm1_g, w_in=w_in, rw_mu=rw_mu, rw_w0=rw_w0, rw_w_up=rw_w_up, rw_a0=rw_a0,
        rw_a_up=rw_a_up, rw_g_up=rw_g_up, rw_k_k=rw_k_k, rw_k_a=rw_k_a, rw_r_k=rw_r_k,
        rw_ln_g=rw_ln_g, rw_ln_b=rw_ln_b, mla_q_norm_g=mla_q_norm_g, mla_w_uq=mla_w_uq,
        mla_kv_norm_g=mla_kv_norm_g, mla_w_uk=mla_w_uk, mla_w_uv=mla_w_uv, mla_qn_g=mla_qn_g,
        mla_kn_g=mla_kn_g, gdn_conv_w=gdn_conv_w, gdn_a_log=gdn_a_log, gdn_dt_bias=gdn_dt_bias,
        gdn_norm_g=gdn_norm_g, dif_qn_g=dif_qn_g, dif_kn_g=dif_kn_g, dif_lam=dif_lam,
        dif_subln_g=dif_subln_g, w_branch=w_branch, w_out=w_out)

    def trunk(x, pos, init_state, paged):
        collected = {name: [] for name in STATE_KEYS}
        for l in range(DEPTH):
            lp = {name: arr[l] for name, arr in layer_params.items()}
            st = {name: arr[l] for name, arr in init_state.items()}
            x, new = _mixer_layer(x, pos, l, lp, rel_bias, st, paged)
            h2 = _rmsnorm(x, norm2_g[l])
            if l % 2 == 0:
                f = _swiglu(h2, ffn_w_gate[l // 2], ffn_w_up[l // 2], ffn_w_down[l // 2])
            else:
                f = _moe(h2, moe_router[l // 2], moe_w_gate[l // 2], moe_w_up[l // 2], moe_w_down[l // 2])
            x = x + f.astype(x.dtype)
            for name in STATE_KEYS:
                collected[name].append(new[name])
        return x, {name: jnp.stack(v, axis=0) for name, v in collected.items()}

    b, dt = x_prompt.shape[0], x_prompt.dtype
    prompt_init = dict(
        rwkv=jnp.zeros((DEPTH, b, RW_HEADS, RW_HEAD, RW_HEAD), dt),
        rwkv_shift=jnp.zeros((DEPTH, b, RW_SHIFT_W), dt),
        gdn=jnp.zeros((DEPTH, b, GDN_HEADS, GDN_DK, GDN_DV), dt),
        gdn_conv=jnp.zeros((DEPTH, b, GDN_CONV - 1, GDN_QKV_W), dt))
    sample_init = dict(rwkv=state_rwkv, rwkv_shift=state_rwkv_shift, gdn=state_gdn, gdn_conv=state_gdn_conv)
    paged = (page_table, cache_mla_ckv, cache_mla_krope, cache_diff_k, cache_diff_v)

    y_prompt, sp = trunk(x_prompt, jnp.arange(x_prompt.shape[1]), prompt_init, None)
    y_sample, ss = trunk(x_sample, PAST_LEN + jnp.arange(x_sample.shape[1]), sample_init, paged)
    return (y_prompt, y_sample,
            sp['mla_ckv'], sp['mla_krope'], sp['diff_k'], sp['diff_v'],
            sp['rwkv'], sp['rwkv_shift'], sp['gdn'], sp['gdn_conv'],
            ss['mla_ckv'], ss['mla_krope'], ss['diff_k'], ss['diff_v'],
            ss['rwkv'], ss['rwkv_shift'], ss['gdn'], ss['gdn_conv'])
```

```python
import functools
import math

import jax
import jax.numpy as jnp
from jax import lax
from jax.experimental import pallas as pl
from jax.experimental.pallas import tpu as pltpu

F32 = jnp.float32
BF16 = jnp.bfloat16
HIGHEST = lax.Precision.HIGHEST
NEG = -1e30

D_MODEL = 1024
DEPTH = 2
PAST_LEN = 16384
PAGE_SIZE = 128
EPS = 1e-6
N_BRANCH = 4
BRANCH_W = 256
HEADS = 4
HEAD_W = 64
RW_DECAY_LORA, RW_AAA_LORA, RW_GATE_LORA = 32, 32, 64
RW_SHIFT_W = 3 * BRANCH_W + RW_DECAY_LORA + RW_AAA_LORA + RW_GATE_LORA
RW_LN_EPS = 64e-5
MLA_NOPE, MLA_ROPE = 64, 32
MLA_QK = MLA_NOPE + MLA_ROPE
MLA_Q_LORA, MLA_KV_LORA = 256, 128
ROPE_BASE = 10000.0
GDN_CONV = 4
GDN_CHUNK = 64
GDN_QKV_W = 768
DIFF_D = 32
DIFF_QKV_W = 768
REL_BUCKETS, REL_MAX_DIST = 32, 128
N_EXPERTS, TOP_K = 8, 2
PROJ_SIZES = (N_BRANCH * D_MODEL, RW_SHIFT_W, MLA_Q_LORA, MLA_KV_LORA, MLA_ROPE,
              GDN_QKV_W, HEADS, HEADS, BRANCH_W, DIFF_QKV_W)
Z_GATE, Z_RW, Z_MQ, Z_MKV, Z_QKV, Z_ZZ, Z_DIF, Z_SMALL = 0, 4096, 4992, 5248, 5376, 6144, 6400, 7168
Z_W = 7296

VMEM_LIMIT = 56 * 1024 * 1024
LANES = 128
SUBLANES = 8


def _params(sem, vmem=VMEM_LIMIT):
    return pltpu.CompilerParams(dimension_semantics=sem, vmem_limit_bytes=vmem)


def _row_tile(m, target):
    if m <= target:
        return m
    best = None
    for t in range(SUBLANES, target + 1, SUBLANES):
        if m % t == 0:
            best = t
    assert best is not None, (m, target)
    return best


def _mm_kernel(x_ref, w_ref, o_ref, *, precision):
    x, w = x_ref[...], w_ref[...]
    if precision is None:
        x, w = x.astype(BF16), w.astype(BF16)
    o_ref[...] = jnp.dot(x, w, preferred_element_type=F32, precision=precision)


def _matmul(x, w, *, tm=512, tn=None, precision=None, name="matmul"):
    m, k = x.shape
    n = w.shape[1]
    tm = _row_tile(m, tm)
    tn = tn or n
    return pl.pallas_call(
        functools.partial(_mm_kernel, precision=precision),
        out_shape=jax.ShapeDtypeStruct((m, n), F32),
        grid=(n // tn, m // tm),
        in_specs=[pl.BlockSpec((tm, k), lambda j, i: (i, 0)),
                  pl.BlockSpec((k, tn), lambda j, i: (0, j))],
        out_specs=pl.BlockSpec((tm, tn), lambda j, i: (i, j)),
        compiler_params=_params(("parallel", "parallel")),
        name=name,
    )(x, w)


def _rms_rows(x, g):
    return x * lax.rsqrt(jnp.mean(x * x, axis=-1, keepdims=True) + EPS) * g


def _rms_mm_kernel(x_ref, g_ref, w_ref, o_ref, *, precision):
    h = _rms_rows(x_ref[...], g_ref[...])
    w = w_ref[...]
    if precision is None:
        h, w = h.astype(BF16), w.astype(BF16)
    o_ref[...] = jnp.dot(h, w, preferred_element_type=F32, precision=precision)


def _rms_matmul(x, g, w, *, tm, tn=None, precision=None, name="rms_matmul"):
    m, k = x.shape
    n = w.shape[1]
    tm = _row_tile(m, tm)
    tn = tn or n
    return pl.pallas_call(
        functools.partial(_rms_mm_kernel, precision=precision),
        out_shape=jax.ShapeDtypeStruct((m, n), F32),
        grid=(n // tn, m // tm),
        in_specs=[pl.BlockSpec((tm, k), lambda j, i: (i, 0)),
                  pl.BlockSpec((1, k), lambda j, i: (0, 0)),
                  pl.BlockSpec((k, tn), lambda j, i: (0, j))],
        out_specs=pl.BlockSpec((tm, tn), lambda j, i: (i, j)),
        compiler_params=_params(("parallel", "parallel")),
        name=name,
    )(x, g.reshape(1, k), w)


def _merge_kernel(x_ref, zg_ref, br_ref, wb_ref, wo_ref, o_ref):
    br = br_ref[...].astype(BF16)
    merged = None
    for n in range(N_BRANCH):
        gate = jax.nn.sigmoid(zg_ref[:, n * D_MODEL:(n + 1) * D_MODEL])
        p = jnp.dot(br[:, n * BRANCH_W:(n + 1) * BRANCH_W], wb_ref[n], preferred_element_type=F32)
        merged = gate * p if merged is None else merged + gate * p
    o_ref[...] = x_ref[...] + jnp.dot(merged.astype(BF16), wo_ref[...], preferred_element_type=F32)


def _merge_out(x, z, br, wb, wo, *, tm=528):
    m = x.shape[0]
    tm = _row_tile(m, tm)
    return pl.pallas_call(
        _merge_kernel,
        out_shape=jax.ShapeDtypeStruct((m, D_MODEL), F32),
        grid=(m // tm,),
        in_specs=[pl.BlockSpec((tm, D_MODEL), lambda i: (i, 0)),
                  pl.BlockSpec((tm, N_BRANCH * D_MODEL), lambda i: (i, 0)),
                  pl.BlockSpec((tm, N_BRANCH * BRANCH_W), lambda i: (i, 0)),
                  pl.BlockSpec((N_BRANCH, BRANCH_W, D_MODEL), lambda i: (0, 0, 0)),
                  pl.BlockSpec((D_MODEL, D_MODEL), lambda i: (0, 0))],
        out_specs=pl.BlockSpec((tm, D_MODEL), lambda i: (i, 0)),
        compiler_params=_params(("parallel",)),
        name="merge_out",
    )(x, z, br, wb, wo)


def _ffn_kernel(x_ref, g_ref, c_ref, wg_ref, wu_ref, wd_ref, o_ref, h_sc, acc_sc):
    e, j = pl.program_id(1), pl.program_id(2)

    @pl.when((e == 0) & (j == 0))
    def _():
        h_sc[...] = _rms_rows(x_ref[...], g_ref[...]).astype(BF16)
        acc_sc[...] = jnp.zeros_like(acc_sc)

    h = h_sc[...]
    a = jnp.dot(h, wg_ref[0], preferred_element_type=F32)
    u = jnp.dot(h, wu_ref[0], preferred_element_type=F32)
    act = (jax.nn.silu(a) * u).astype(BF16)
    acc_sc[...] += c_ref[0] * jnp.dot(act, wd_ref[0], preferred_element_type=F32)

    @pl.when((e == pl.num_programs(1) - 1) & (j == pl.num_programs(2) - 1))
    def _():
        o_ref[...] = x_ref[...] + acc_sc[...]


def _ffn(x, g, combine, wg, wu, wd, *, tm, tf):
    m = x.shape[0]
    n_e, _, f = wg.shape
    tm = _row_tile(m, tm)
    return pl.pallas_call(
        _ffn_kernel,
        out_shape=jax.ShapeDtypeStruct((m, D_MODEL), F32),
        grid=(m // tm, n_e, f // tf),
        in_specs=[pl.BlockSpec((tm, D_MODEL), lambda i, e, j: (i, 0)),
                  pl.BlockSpec((1, D_MODEL), lambda i, e, j: (0, 0)),
                  pl.BlockSpec((1, tm, 1), lambda i, e, j: (e, i, 0)),
                  pl.BlockSpec((1, D_MODEL, tf), lambda i, e, j: (e, 0, j)),
                  pl.BlockSpec((1, D_MODEL, tf), lambda i, e, j: (e, 0, j)),
                  pl.BlockSpec((1, tf, D_MODEL), lambda i, e, j: (e, j, 0))],
        out_specs=pl.BlockSpec((tm, D_MODEL), lambda i, e, j: (i, 0)),
        scratch_shapes=[pltpu.VMEM((tm, D_MODEL), BF16), pltpu.VMEM((tm, D_MODEL), F32)],
        compiler_params=_params(("parallel", "arbitrary", "arbitrary")),
        name="ffn",
    )(x, g.reshape(1, D_MODEL), combine, wg, wu, wd)


def _flash_kernel(q_ref, k_ref, v_ref, b_ref, o_ref, m_sc, l_sc, acc_sc, *, t):
    qi = pl.program_id(2)
    q = q_ref[0, 0]
    m_sc[...] = jnp.full_like(m_sc, NEG)
    l_sc[...] = jnp.zeros_like(l_sc)
    acc_sc[...] = jnp.zeros_like(acc_sc)

    def body(kj, carry):
        off = pl.multiple_of(kj * t, t)
        k = k_ref[0, 0, pl.ds(off, t), :]
        v = v_ref[0, 0, pl.ds(off, t), :]
        s = lax.dot_general(q, k, (((1,), (1,)), ((), ())), preferred_element_type=F32)
        s = s + b_ref[0, jnp.minimum(qi - kj, 2)]
        m_prev = m_sc[...]
        m_new = jnp.maximum(m_prev, jnp.max(s, axis=-1, keepdims=True))
        alpha = jnp.exp(m_prev - m_new)
        p = jnp.exp(s - m_new)
        l_sc[...] = alpha * l_sc[...] + jnp.sum(p, axis=-1, keepdims=True)
        acc_sc[...] = alpha * acc_sc[...] + jnp.dot(p.astype(BF16), v, preferred_element_type=F32)
        m_sc[...] = m_new
        return carry

    lax.fori_loop(0, qi + 1, body, 0)
    o_ref[0, 0] = acc_sc[...] / l_sc[...]


def _flash(q, k, v, bias, *, t):
    b, n_map, s, d = q.shape
    hv, dv = v.shape[1], v.shape[3]
    rep = n_map // hv
    bias_idx = (lambda bi, h, i: (h // rep, 0, 0, 0)) if bias.shape[0] > 1 else (lambda bi, h, i: (0, 0, 0, 0))
    return pl.pallas_call(
        functools.partial(_flash_kernel, t=t),
        out_shape=jax.ShapeDtypeStruct((b, n_map, s, dv), F32),
        grid=(b, n_map, s // t),
        in_specs=[pl.BlockSpec((1, 1, t, d), lambda bi, h, i: (bi, h, i, 0)),
                  pl.BlockSpec((1, 1, s, d), lambda bi, h, i: (bi, h, 0, 0)),
                  pl.BlockSpec((1, 1, s, dv), lambda bi, h, i: (bi, h // rep, 0, 0)),
                  pl.BlockSpec((1, 3, t, t), bias_idx)],
        out_specs=pl.BlockSpec((1, 1, t, dv), lambda bi, h, i: (bi, h, i, 0)),
        scratch_shapes=[pltpu.VMEM((t, 1), F32), pltpu.VMEM((t, 1), F32), pltpu.VMEM((t, dv), F32)],
        compiler_params=_params(("parallel", "parallel", "arbitrary")),
        name="flash",
    )(q, k, v, bias)


def _gdn_kernel(q_ref, k_ref, v_ref, gcol_ref, grow_ref, bcol_ref, s0_ref, o_ref, sout_ref, s_sc, *, bb, c):
    @pl.when(pl.program_id(1) == 0)
    def _():
        s_sc[...] = s0_ref[...]

    ii = lax.broadcasted_iota(jnp.int32, (c, c), 0)
    jj = lax.broadcasted_iota(jnp.int32, (c, c), 1)
    causal = ii >= jj
    tril = causal.astype(F32)
    triu = (ii <= jj).astype(F32)
    last = (((1,), (1,)), ((), ()))
    first = (((0,), (0,)), ((), ()))
    dot = functools.partial(jnp.dot, preferred_element_type=F32, precision=HIGHEST)
    for b in range(bb):
        for h in range(HEADS):
            q = q_ref[b, h] * (HEAD_W ** -0.5)
            k, v = k_ref[b, h], v_ref[b, h]
            beta = bcol_ref[b, h, 0]
            gc_c = dot(tril, jnp.broadcast_to(gcol_ref[b, h, 0], (c, c)))
            gc_r = dot(jnp.broadcast_to(grow_ref[b, h, 0], (c, c)), triu)
            decay = jnp.where(causal, jnp.exp(jnp.where(causal, gc_c - gc_r, 0.0)), 0.0)
            gc = gc_c[:, 0:1]
            gc_last = gc_c[c - 1:c, 0:1]
            kb = k * beta
            a_low = jnp.where(ii > jj, lax.dot_general(kb, k, last, preferred_element_type=F32,
                                                       precision=HIGHEST) * decay, 0.0)
            x = jnp.concatenate([v * beta, kb * jnp.exp(gc)], axis=1)
            for j in range(c - 1):
                x = x - a_low[:, j:j + 1] * x[j:j + 1, :]
            u, w = x[:, :HEAD_W], x[:, HEAD_W:]
            s = s_sc[b, h]
            v_new = u - dot(w, s)
            qk = lax.dot_general(q, k, last, preferred_element_type=F32, precision=HIGHEST) * decay
            o_ref[b, h] = dot(q * jnp.exp(gc), s) + dot(qk, v_new)
            kg = k * jnp.exp(gc_last - gc)
            s_sc[b, h] = s * jnp.exp(gc_last) + lax.dot_general(kg, v_new, first, preferred_element_type=F32,
                                                                precision=HIGHEST)
    sout_ref[...] = s_sc[...]


def _gdn_chunks(q, k, v, g, beta, s0, *, c, bb):
    b, h, l, _ = q.shape
    n = l // c
    gcol = g.reshape(b, h, n, c, 1)
    grow = g.reshape(b, h, n, 1, c)
    bcol = beta.reshape(b, h, n, c, 1)
    seq = pl.BlockSpec((bb, h, c, HEAD_W), lambda i, j: (i, 0, j, 0))
    col = pl.BlockSpec((bb, h, 1, c, 1), lambda i, j: (i, 0, j, 0, 0))
    row = pl.BlockSpec((bb, h, 1, 1, c), lambda i, j: (i, 0, j, 0, 0))
    st = pl.BlockSpec((bb, h, HEAD_W, HEAD_W), lambda i, j: (i, 0, 0, 0))
    return pl.pallas_call(
        functools.partial(_gdn_kernel, bb=bb, c=c),
        out_shape=(jax.ShapeDtypeStruct((b, h, l, HEAD_W), F32),
                   jax.ShapeDtypeStruct((b, h, HEAD_W, HEAD_W), F32)),
        grid=(b // bb, n),
        in_specs=[seq, seq, seq, col, row, col, st],
        out_specs=(seq, st),
        scratch_shapes=[pltpu.VMEM((bb, h, HEAD_W, HEAD_W), F32)],
        compiler_params=_params(("parallel", "arbitrary")),
        name="gdn_chunks",
    )(q, k, v, gcol, grow, bcol, s0)


def _rwkv_kernel(r_ref, w_ref, k_ref, v_ref, kk_ref, kka_ref, s0_ref, y_ref, sout_ref, s_sc, *, bb, tc, grp):
    @pl.when(pl.program_id(1) == 0)
    def _():
        s_sc[...] = s0_ref[...]

    ones_blk = (lax.broadcasted_iota(jnp.int32, (BRANCH_W, BRANCH_W), 0) // HEAD_W ==
                lax.broadcasted_iota(jnp.int32, (BRANCH_W, BRANCH_W), 1) // HEAD_W).astype(F32)
    dmask = (lax.broadcasted_iota(jnp.int32, (HEAD_W, BRANCH_W), 0) ==
             lax.broadcasted_iota(jnp.int32, (HEAD_W, BRANCH_W), 1) % HEAD_W).astype(F32)
    seg = functools.partial(jnp.dot, preferred_element_type=F32, precision=HIGHEST)

    def group(gi, carry):
        base = gi * grp
        for t in range(grp):
            for b in range(bb):
                row = lambda ref: ref[b, pl.ds(base + t, 1), :]
                s = s_sc[b]
                v_b = seg(dmask * row(v_ref), ones_blk)
                s_kk = seg(s * row(kk_ref), ones_blk)
                s = s * row(w_ref) - s_kk * row(kka_ref) + v_b * row(k_ref)
                y_b = seg(s * row(r_ref), ones_blk)
                y_ref[b, pl.ds(base + t, 1), :] = jnp.sum(dmask * y_b, axis=0, keepdims=True)
                s_sc[b] = s
        return carry

    lax.fori_loop(0, tc // grp, group, 0)
    sout_ref[...] = s_sc[...]


def _rwkv_scan(r, w, k, v, kk, kka, s0, *, bb, tc):
    b, l, _ = r.shape
    grp = math.gcd(tc, 8)
    seq = pl.BlockSpec((bb, tc, BRANCH_W), lambda i, j: (i, j, 0))
    st = pl.BlockSpec((bb, HEAD_W, BRANCH_W), lambda i, j: (i, 0, 0))
    return pl.pallas_call(
        functools.partial(_rwkv_kernel, bb=bb, tc=tc, grp=grp),
        out_shape=(jax.ShapeDtypeStruct((b, l, BRANCH_W), F32),
                   jax.ShapeDtypeStruct((b, HEAD_W, BRANCH_W), F32)),
        grid=(b // bb, l // tc),
        in_specs=[seq] * 6 + [st],
        out_specs=(seq, st),
        scratch_shapes=[pltpu.VMEM((bb, HEAD_W, BRANCH_W), F32)],
        compiler_params=_params(("parallel", "arbitrary")),
        name="rwkv_scan",
    )(r, w, k, v, kk, kka, s0)


def _page_copy(pool, buf, sem, layer, page, slot, p):
    return pltpu.make_async_copy(pool.at[layer, page], buf.at[slot, pl.ds(p * PAGE_SIZE, PAGE_SIZE), :], sem.at[slot])


def _online_softmax_step(s, values, m_sc, l_sc, acc_sc):
    m_prev = m_sc[...]
    m_new = jnp.maximum(m_prev, jnp.max(s, axis=-1, keepdims=True))
    alpha = jnp.exp(m_prev - m_new)
    p = jnp.exp(s - m_new)
    l_sc[...] = alpha * l_sc[...] + jnp.sum(p, axis=-1, keepdims=True)
    acc_sc[...] = alpha * acc_sc[...] + jnp.dot(p.astype(BF16), values, preferred_element_type=F32)
    m_sc[...] = m_new


def _mla_dec_kernel(pt_ref, qa_ref, qr_ref, wt_ref, wuv_ref, cn_ref, rn_ref, mn_ref, ckv_hbm, kr_hbm, o_ref,
                    cbuf, rbuf, sem_c, sem_r, m_sc, l_sc, acc_sc, *, layer, pc, n_chunks):
    b = pl.program_id(0)
    last = (((1,), (1,)), ((), ()))
    rows = HEADS * SUBLANES

    def fetch(chunk, slot):
        for p in range(pc):
            page = pt_ref[b, chunk * pc + p]
            _page_copy(ckv_hbm, cbuf, sem_c, layer, page, slot, p).start()
            _page_copy(kr_hbm, rbuf, sem_r, layer, page, slot, p).start()

    def wait(slot):
        for p in range(pc):
            _page_copy(ckv_hbm, cbuf, sem_c, layer, 0, slot, p).wait()
            _page_copy(kr_hbm, rbuf, sem_r, layer, 0, slot, p).wait()

    def process(c, r, mask):
        cb, rb = c.astype(BF16), r.astype(BF16)
        knt = lax.dot_general(wt_ref[...], cb, last, preferred_element_type=F32)
        kn2 = knt * knt
        r2 = r * r
        r2h = r2.astype(BF16)
        r2l = (r2 - r2h.astype(F32)).astype(BF16)
        ones = jnp.ones((SUBLANES, MLA_ROPE), BF16)
        ssr = (lax.dot_general(ones, r2h, last, preferred_element_type=F32)
               + lax.dot_general(ones, r2l, last, preferred_element_type=F32))[0:1]
        s_raw = (lax.dot_general(qa_ref[0], cb, last, preferred_element_type=F32)
                 + lax.dot_general(qr_ref[0], rb, last, preferred_element_type=F32))
        parts = []
        for h in range(HEADS):
            ss = jnp.sum(kn2[h * HEAD_W:(h + 1) * HEAD_W], axis=0, keepdims=True) + ssr
            rinv = lax.rsqrt(ss * (1.0 / MLA_QK) + EPS)
            parts.append(s_raw[h * SUBLANES:(h + 1) * SUBLANES] * rinv)
        s = jnp.concatenate(parts, axis=0)
        if mask is not None:
            s = s + mask
        _online_softmax_step(s, cb, m_sc, l_sc, acc_sc)

    m_sc[...] = jnp.full_like(m_sc, NEG)
    l_sc[...] = jnp.zeros_like(l_sc)
    acc_sc[...] = jnp.zeros_like(acc_sc)
    fetch(0, 0)
    for ch in range(n_chunks):
        slot = ch % 2
        if ch + 1 < n_chunks:
            fetch(ch + 1, 1 - slot)
        wait(slot)
        process(cbuf[slot], rbuf[slot], None)
    process(cn_ref[0], rn_ref[0], mn_ref[...])
    ctx = (acc_sc[...] / l_sc[...]).astype(BF16)
    for h in range(HEADS):
        o_ref[0, h * SUBLANES:(h + 1) * SUBLANES, :] = jnp.dot(
            ctx[h * SUBLANES:(h + 1) * SUBLANES], wuv_ref[h], preferred_element_type=F32)


def _mla_decode(page_table, qa, qr, wt, wuv, c_new, r_new, mask_new, ckv_pool, kr_pool, *, layer, pc=16):
    b, n_pages = page_table.shape
    rows = HEADS * SUBLANES
    n_chunks = n_pages // pc
    n = pc * PAGE_SIZE
    per_seq = lambda shape: pl.BlockSpec((1,) + shape, lambda i, pt: (i, 0, 0))
    whole = lambda shape: pl.BlockSpec(shape, lambda i, pt: (0,) * len(shape))
    return pl.pallas_call(
        functools.partial(_mla_dec_kernel, layer=layer, pc=pc, n_chunks=n_chunks),
        out_shape=jax.ShapeDtypeStruct((b, rows, HEAD_W), F32),
        grid_spec=pltpu.PrefetchScalarGridSpec(
            num_scalar_prefetch=1,
            grid=(b,),
            in_specs=[per_seq((rows, MLA_KV_LORA)), per_seq((rows, MLA_ROPE)),
                      whole((BRANCH_W, MLA_KV_LORA)), whole((HEADS, MLA_KV_LORA, HEAD_W)),
                      per_seq((SUBLANES, MLA_KV_LORA)), per_seq((SUBLANES, MLA_ROPE)),
                      whole((rows, SUBLANES)),
                      pl.BlockSpec(memory_space=pl.ANY), pl.BlockSpec(memory_space=pl.ANY)],
            out_specs=per_seq((rows, HEAD_W)),
            scratch_shapes=[pltpu.VMEM((2, n, MLA_KV_LORA), F32), pltpu.VMEM((2, n, MLA_ROPE), F32),
                            pltpu.SemaphoreType.DMA((2,)), pltpu.SemaphoreType.DMA((2,)),
                            pltpu.VMEM((rows, 1), F32), pltpu.VMEM((rows, 1), F32),
                            pltpu.VMEM((rows, MLA_KV_LORA), F32)]),
        compiler_params=_params(("arbitrary",)),
        name="mla_decode",
    )(page_table, qa, qr, wt, wuv, c_new, r_new, mask_new, ckv_pool, kr_pool)


def _dif_dec_kernel(pt_ref, q_ref, kn_ref, vn_ref, bfar_ref, blast_ref, bnew_ref, dk_hbm, dv_hbm, o_ref,
                    kbuf, vbuf, sem_k, sem_v, m_sc, l_sc, acc_sc, *, layer, pc, n_chunks):
    b = pl.program_id(0)
    last = (((1,), (1,)), ((), ()))

    def fetch(chunk, slot):
        for p in range(pc):
            page = pt_ref[b, chunk * pc + p]
            _page_copy(dk_hbm, kbuf, sem_k, layer, page, slot, p).start()
            _page_copy(dv_hbm, vbuf, sem_v, layer, page, slot, p).start()

    def wait(slot):
        for p in range(pc):
            _page_copy(dk_hbm, kbuf, sem_k, layer, 0, slot, p).wait()
            _page_copy(dv_hbm, vbuf, sem_v, layer, 0, slot, p).wait()

    def process(k, v, bias):
        s = lax.dot_general(q_ref[0], k.astype(BF16), last, preferred_element_type=F32) + bias
        _online_softmax_step(s, v.astype(BF16), m_sc, l_sc, acc_sc)

    m_sc[...] = jnp.full_like(m_sc, NEG)
    l_sc[...] = jnp.zeros_like(l_sc)
    acc_sc[...] = jnp.zeros_like(acc_sc)
    fetch(0, 0)
    for ch in range(n_chunks):
        slot = ch % 2
        if ch + 1 < n_chunks:
            fetch(ch + 1, 1 - slot)
        wait(slot)
        process(kbuf[slot], vbuf[slot], blast_ref[...] if ch == n_chunks - 1 else bfar_ref[...])
    process(kn_ref[0], vn_ref[0], bnew_ref[...])
    o_ref[0] = acc_sc[...] / l_sc[...]


def _dif_decode(page_table, q_bd, k_new, v_new, bias_far, bias_last, bias_new, dk_pool, dv_pool, *, layer, pc=16):
    b, n_pages = page_table.shape
    rows = q_bd.shape[1]
    n_chunks = n_pages // pc
    n = pc * PAGE_SIZE
    per_seq = lambda shape: pl.BlockSpec((1,) + shape, lambda i, pt: (i, 0, 0))
    whole = lambda shape: pl.BlockSpec(shape, lambda i, pt: (0,) * len(shape))
    return pl.pallas_call(
        functools.partial(_dif_dec_kernel, layer=layer, pc=pc, n_chunks=n_chunks),
        out_shape=jax.ShapeDtypeStruct((b, rows, BRANCH_W), F32),
        grid_spec=pltpu.PrefetchScalarGridSpec(
            num_scalar_prefetch=1,
            grid=(b,),
            in_specs=[per_seq((rows, BRANCH_W)), per_seq((SUBLANES, BRANCH_W)), per_seq((SUBLANES, BRANCH_W)),
                      whole((rows, 1)), whole((rows, n)), whole((rows, SUBLANES)),
                      pl.BlockSpec(memory_space=pl.ANY), pl.BlockSpec(memory_space=pl.ANY)],
            out_specs=per_seq((rows, BRANCH_W)),
            scratch_shapes=[pltpu.VMEM((2, n, BRANCH_W), F32), pltpu.VMEM((2, n, BRANCH_W), F32),
                            pltpu.SemaphoreType.DMA((2,)), pltpu.SemaphoreType.DMA((2,)),
                            pltpu.VMEM((rows, 1), F32), pltpu.VMEM((rows, 1), F32),
                            pltpu.VMEM((rows, BRANCH_W), F32)]),
        compiler_params=_params(("arbitrary",)),
        name="dif_decode",
    )(page_table, q_bd, k_new, v_new, bias_far, bias_last, bias_new, dk_pool, dv_pool)


def _rms(x, g):
    return x * lax.rsqrt(jnp.mean(x * x, axis=-1, keepdims=True) + EPS) * g


def _l2(x):
    return x * lax.rsqrt(jnp.maximum(jnp.sum(x * x, axis=-1, keepdims=True), 1e-12))


def _rope(x, pos):
    half = x.shape[-1] // 2
    inv = ROPE_BASE ** (-jnp.arange(half, dtype=F32) / half)
    ang = pos.astype(F32)[:, None] * inv[None, :]
    cos, sin = jnp.cos(ang)[:, None, :], jnp.sin(ang)[:, None, :]
    x1, x2 = x[..., :half], x[..., half:]
    return jnp.concatenate([x1 * cos - x2 * sin, x1 * sin + x2 * cos], axis=-1)


def _rel_bias(rel_table, qpos, kpos):
    n = jnp.maximum(qpos[:, None] - kpos[None, :], 0)
    exact = REL_BUCKETS // 2
    nf = jnp.maximum(n, 1).astype(F32)
    large = exact + (jnp.log(nf / exact) / math.log(REL_MAX_DIST / exact) * (REL_BUCKETS - exact)).astype(jnp.int32)
    bucket = jnp.where(n < exact, n, jnp.minimum(large, REL_BUCKETS - 1))
    return jnp.moveaxis(rel_table[bucket], -1, 0).astype(F32)


def _pad_rows(x, axis, to):
    pad = [(0, 0)] * x.ndim
    pad[axis] = (0, to - x.shape[axis])
    return jnp.pad(x, pad)


def _pack_w_in(w_in):
    o, parts = 0, []
    for wdt in PROJ_SIZES:
        parts.append(w_in[:, o:o + wdt])
        o += wdt
    zg, zrw, zmq, zmkv, zmkr, zqkv, zb, za, zz, zdif = parts
    small = jnp.concatenate([zmkr, zb, za], axis=1)
    small = jnp.pad(small, ((0, 0), (0, Z_W - Z_SMALL - small.shape[1])))
    return jnp.concatenate([zg, zrw, zmq, zmkv, zqkv, zz, zdif, small], axis=1).astype(BF16)


def _rwkv_branch(zr, shift_prev, s0, lp, *, bb, tc):
    b, l, _ = zr.shape
    prev = jnp.concatenate([shift_prev[:, None], zr[:, :-1]], axis=1)
    zs = (zr + (prev - zr) * lp['rw_mu']).reshape(b * l, RW_SHIFT_W)
    o1, o2, o3 = BRANCH_W, 2 * BRANCH_W, 3 * BRANCH_W
    r, k, v = zs[:, :o1], zs[:, o1:o2], zs[:, o2:o3]
    wd = zs[:, o3:o3 + RW_DECAY_LORA]
    ad = zs[:, o3 + RW_DECAY_LORA:o3 + RW_DECAY_LORA + RW_AAA_LORA]
    gd = zs[:, o3 + RW_DECAY_LORA + RW_AAA_LORA:]
    tm = 2048
    w_log = -jax.nn.softplus(-(lp['rw_w0'] + _matmul(jnp.tanh(wd), lp['rw_w_up'], tm=tm, precision=HIGHEST))) - 0.5
    decay = jnp.exp(-jnp.exp(w_log))
    a = jax.nn.sigmoid(lp['rw_a0'] + _matmul(ad, lp['rw_a_up'], tm=tm, precision=HIGHEST))
    g = _matmul(jax.nn.sigmoid(gd), lp['rw_g_up'], tm=tm, precision=HIGHEST)
    heads = lambda t: t.reshape(b * l, HEADS, HEAD_W)
    kk = _l2(heads(k * lp['rw_k_k'])).reshape(b * l, BRANCH_W)
    k = k * (1.0 + (a - 1.0) * lp['rw_k_a'])
    seq = lambda t: t.reshape(b, l, BRANCH_W)
    s0_t = jnp.transpose(s0, (0, 2, 1, 3)).reshape(b, HEAD_W, BRANCH_W)
    y, s_new = _rwkv_scan(seq(r), seq(decay), seq(k), seq(v), seq(kk), seq(kk * a), s0_t, bb=bb, tc=tc)
    s_new = jnp.transpose(s_new.reshape(b, HEAD_W, HEADS, HEAD_W), (0, 2, 1, 3))
    y = heads(y.reshape(b * l, BRANCH_W))
    mu = jnp.mean(y, axis=-1, keepdims=True)
    var = jnp.mean(jnp.square(y - mu), axis=-1, keepdims=True)
    y = ((y - mu) * lax.rsqrt(var + RW_LN_EPS) * lp['rw_ln_g'].reshape(HEADS, HEAD_W)
         + lp['rw_ln_b'].reshape(HEADS, HEAD_W))
    y = y + jnp.sum(heads(r) * heads(k) * lp['rw_r_k'], axis=-1, keepdims=True) * heads(v)
    return (y.reshape(b * l, BRANCH_W) * g).reshape(b, l, BRANCH_W), s_new, zr[:, -1]


def _gdn_branch(zqkv, zb, za, zz, conv_prev, s0, lp, *, bb):
    b, l, _ = zqkv.shape
    xc = jnp.concatenate([conv_prev, zqkv], axis=1)
    wc = lp['gdn_conv_w']
    conv = sum(xc[:, j:j + l] * wc[j] for j in range(GDN_CONV))
    qkv = jax.nn.silu(conv)
    hm = lambda t: jnp.transpose(t.reshape(b, l, HEADS, HEAD_W), (0, 2, 1, 3))
    q = _l2(hm(qkv[..., :BRANCH_W]))
    k = _l2(hm(qkv[..., BRANCH_W:2 * BRANCH_W]))
    v = hm(qkv[..., 2 * BRANCH_W:])
    beta = jnp.transpose(jax.nn.sigmoid(zb), (0, 2, 1))
    g = jnp.transpose(-jnp.exp(lp['gdn_a_log']) * jax.nn.softplus(za + lp['gdn_dt_bias']), (0, 2, 1))
    c = math.gcd(l, GDN_CHUNK)
    lpad = l
    if c < SUBLANES:
        lpad = c = SUBLANES
        q, k, v = (_pad_rows(t, 2, lpad) for t in (q, k, v))
        g, beta = _pad_rows(g, 2, lpad), _pad_rows(beta, 2, lpad)
    o, s_new = _gdn_chunks(q, k, v, g, beta, s0, c=c, bb=bb)
    o = jnp.transpose(o[:, :, :l], (0, 2, 1, 3))
    o = _rms(o, lp['gdn_norm_g']) * jax.nn.silu(zz.reshape(b, l, HEADS, HEAD_W))
    return o.reshape(b, l, BRANCH_W), s_new, xc[:, -(GDN_CONV - 1):]


def _mla_project(zmq, zmkv, zmkr, pos, lp):
    b, l, _ = zmq.shape
    cq = _rms(zmq, lp['mla_q_norm_g']).reshape(b * l, MLA_Q_LORA)
    q = _matmul(cq, lp['mla_w_uq'].reshape(MLA_Q_LORA, HEADS * MLA_QK), tm=2048).reshape(b, l, HEADS, MLA_QK)
    q = jnp.concatenate([q[..., :MLA_NOPE], _rope(q[..., MLA_NOPE:], pos)], axis=-1)
    q = _rms(q, lp['mla_qn_g'])
    ckv = _rms(zmkv, lp['mla_kv_norm_g'])
    kr = _rope(zmkr[:, :, None, :], pos)[:, :, 0]
    return q, ckv, kr


def _diff_project(zd, lp):
    b, l, _ = zd.shape
    w = HEADS * 2 * DIFF_D
    q = _rms(zd[..., :w].reshape(b, l, HEADS, 2, DIFF_D), lp['dif_qn_g'])
    k = _rms(zd[..., w:2 * w].reshape(b, l, HEADS, 2, DIFF_D), lp['dif_kn_g'])
    return (q.reshape(b, l, HEADS, 2 * DIFF_D), k.reshape(b, l, HEADS, 2 * DIFF_D),
            zd[..., 2 * w:].reshape(b, l, HEADS, HEAD_W))


def _diff_lambda(lam, l):
    lam_init = 0.8 - 0.6 * math.exp(-0.3 * l)
    return jnp.exp(jnp.sum(lam[0] * lam[1])) - jnp.exp(jnp.sum(lam[2] * lam[3])) + lam_init, lam_init


def _prompt_attention(q_m, ckv, kr, q_d, k_d, v_d, lam, lp, rel_table, *, t):
    b, s = q_m.shape[:2]
    flat = ckv.reshape(b * s, MLA_KV_LORA)
    k_nope = _matmul(flat, lp['mla_w_uk'].reshape(MLA_KV_LORA, BRANCH_W), tm=2048).reshape(b, s, HEADS, MLA_NOPE)
    v_m = _matmul(flat, lp['mla_w_uv'].reshape(MLA_KV_LORA, BRANCH_W), tm=2048).reshape(b, s, HEADS, HEAD_W)
    k_rope = jnp.broadcast_to(kr[:, :, None, :], (b, s, HEADS, MLA_ROPE))
    k_m = _rms(jnp.concatenate([k_nope, k_rope], axis=-1), lp['mla_kn_g'])
    hm = lambda x: jnp.transpose(x, (0, 2, 1, 3)).astype(BF16)
    idx = jnp.arange(t)
    causal = jnp.where(idx[None, :] <= idx[:, None], 0.0, NEG).astype(F32)
    zeros = jnp.zeros((t, t), F32)
    o_mla = _flash(hm(q_m * MLA_QK ** -0.5), hm(k_m), hm(v_m), jnp.stack([causal, zeros, zeros])[None], t=t)
    o_mla = jnp.transpose(o_mla, (0, 2, 1, 3)).reshape(b, s, BRANCH_W)
    maps = lambda x: jnp.transpose(x.reshape(b, s, HEADS, 2, DIFF_D), (0, 2, 3, 1, 4)).reshape(b, 2 * HEADS, s, DIFF_D)
    bias = jnp.stack([_rel_bias(rel_table, idx, idx) + causal[None],
                      _rel_bias(rel_table, t + idx, idx),
                      _rel_bias(rel_table, 2 * t + idx, idx)], axis=1)
    o = _flash(maps(q_d * DIFF_D ** -0.5).astype(BF16), maps(k_d).astype(BF16), hm(v_d), bias, t=t)
    o = o.reshape(b, HEADS, 2, s, HEAD_W)
    o_dif = jnp.transpose(o[:, :, 0] - lam * o[:, :, 1], (0, 2, 1, 3))
    return o_mla, o_dif


def _sample_attention(q_m, ckv, kr, q_d, k_d, v_d, lam, lp, rel_table, page_table, caches, layer):
    b, l = q_m.shape[:2]
    ckv_pool, kr_pool, dk_pool, dv_pool = caches
    n_past = page_table.shape[1] * PAGE_SIZE
    pos = n_past + jnp.arange(l)
    tok = jnp.arange(SUBLANES)
    new_ok = (tok[None, :] <= tok[:, None]) & (tok[None, :] < l)
    mask_new = jnp.where(new_ok, 0.0, NEG).astype(F32)
    qg = q_m * lp['mla_kn_g'] * MLA_QK ** -0.5
    w_uk = lp['mla_w_uk']
    eye = jnp.eye(HEADS, dtype=F32)
    w_bd = jnp.einsum('rhd,hg->hdgr', w_uk, eye).reshape(BRANCH_W, HEADS * MLA_KV_LORA)
    qa = _matmul(qg[..., :MLA_NOPE].reshape(b * l, BRANCH_W), w_bd, precision=HIGHEST)
    qa = jnp.transpose(qa.reshape(b, l, HEADS, MLA_KV_LORA), (0, 2, 1, 3))
    qr = jnp.transpose(qg[..., MLA_NOPE:], (0, 2, 1, 3))
    rows = HEADS * SUBLANES
    qa = _pad_rows(qa, 2, SUBLANES).reshape(b, rows, MLA_KV_LORA).astype(BF16)
    qr = _pad_rows(qr, 2, SUBLANES).reshape(b, rows, MLA_ROPE).astype(BF16)
    wt = jnp.transpose(w_uk.reshape(MLA_KV_LORA, BRANCH_W)).astype(BF16)
    wuv = jnp.transpose(lp['mla_w_uv'], (1, 0, 2)).astype(BF16)
    o = _mla_decode(page_table, qa, qr, wt, wuv, _pad_rows(ckv, 1, SUBLANES), _pad_rows(kr, 1, SUBLANES),
                    jnp.tile(mask_new, (HEADS, 1)), ckv_pool, kr_pool, layer=layer)
    o_mla = jnp.transpose(o.reshape(b, HEADS, SUBLANES, HEAD_W)[:, :, :l], (0, 2, 1, 3)).reshape(b, l, BRANCH_W)
    q5 = jnp.transpose(q_d.reshape(b, l, HEADS, 2, DIFF_D), (0, 2, 3, 1, 4)) * DIFF_D ** -0.5
    q5 = _pad_rows(q5, 3, SUBLANES)
    eye2 = jnp.eye(2, dtype=F32)
    q_bd = jnp.einsum('bhmtd,hg,mn->bhmtgnd', q5, eye, eye2).reshape(b, 2 * rows, BRANCH_W).astype(BF16)
    pc = 16
    n_last = pc * PAGE_SIZE
    expand = lambda x: jnp.broadcast_to(_pad_rows(x, 1, SUBLANES)[:, None], (HEADS, 2, SUBLANES, x.shape[-1])
                                        ).reshape(2 * rows, x.shape[-1])
    bias_last = expand(_rel_bias(rel_table, pos, jnp.arange(n_past - n_last, n_past)))
    bias_far = expand(_rel_bias(rel_table, pos, jnp.arange(1)))
    bias_new = _rel_bias(rel_table, n_past + tok, n_past + tok) + mask_new[None]
    bias_new = jnp.broadcast_to(bias_new[:, None], (HEADS, 2, SUBLANES, SUBLANES)).reshape(2 * rows, SUBLANES)
    flat = lambda x: _pad_rows(x.reshape(b, l, BRANCH_W), 1, SUBLANES)
    o = _dif_decode(page_table, q_bd, flat(k_d), flat(v_d), bias_far, bias_last, bias_new,
                    dk_pool.reshape(dk_pool.shape[:3] + (BRANCH_W,)), dv_pool.reshape(dv_pool.shape[:3] + (BRANCH_W,)),
                    layer=layer, pc=pc)
    o = o.reshape(b, HEADS, 2, SUBLANES, HEADS, HEAD_W)
    o = jnp.stack([o[:, h, :, :, h] for h in range(HEADS)], axis=1)
    o_dif = jnp.transpose(o[:, :, 0, :l] - lam * o[:, :, 1, :l], (0, 2, 1, 3))
    return o_mla, o_dif


def _group_branches(z, pos, layer, lp, rel_table, st, paged):
    b, l, _ = z.shape
    zrw = z[..., Z_RW:Z_RW + RW_SHIFT_W]
    zmq, zmkv = z[..., Z_MQ:Z_MQ + MLA_Q_LORA], z[..., Z_MKV:Z_MKV + MLA_KV_LORA]
    zqkv, zz, zdif = z[..., Z_QKV:Z_QKV + GDN_QKV_W], z[..., Z_ZZ:Z_ZZ + BRANCH_W], z[..., Z_DIF:Z_DIF + DIFF_QKV_W]
    zmkr = z[..., Z_SMALL:Z_SMALL + MLA_ROPE]
    zb = z[..., Z_SMALL + MLA_ROPE:Z_SMALL + MLA_ROPE + HEADS]
    za = z[..., Z_SMALL + MLA_ROPE + HEADS:Z_SMALL + MLA_ROPE + 2 * HEADS]
    prompt = paged is None
    o_rw, rw_s, rw_shift = _rwkv_branch(zrw, st['rwkv_shift'], st['rwkv'], lp,
                                        bb=b if prompt else 8, tc=512 if prompt else l)
    o_gdn, gdn_s, gdn_conv = _gdn_branch(zqkv, zb, za, zz, st['gdn_conv'], st['gdn'], lp, bb=2)
    q_m, ckv, kr = _mla_project(zmq, zmkv, zmkr, pos, lp)
    q_d, k_d, v_d = _diff_project(zdif, lp)
    lam, lam_init = _diff_lambda(lp['dif_lam'], layer)
    if prompt:
        o_mla, o_dif = _prompt_attention(q_m, ckv, kr, q_d, k_d, v_d, lam, lp, rel_table, t=512)
    else:
        page_table, caches = paged
        o_mla, o_dif = _sample_attention(q_m, ckv, kr, q_d, k_d, v_d, lam, lp, rel_table, page_table, caches, layer)
    o_dif = (_rms(o_dif, lp['dif_subln_g']) * (1.0 - lam_init)).reshape(b, l, BRANCH_W)
    br = jnp.concatenate([o_rw, o_mla, o_gdn, o_dif], axis=-1).reshape(b * l, N_BRANCH * BRANCH_W)
    new = dict(mla_ckv=ckv, mla_krope=kr, diff_k=k_d, diff_v=v_d,
               rwkv=rw_s, rwkv_shift=rw_shift, gdn=gdn_s, gdn_conv=gdn_conv)
    return br, new


STATE_KEYS = ('mla_ckv', 'mla_krope', 'diff_k', 'diff_v', 'rwkv', 'rwkv_shift', 'gdn', 'gdn_conv')


def kernel(x_prompt, x_sample, cache_mla_ckv, cache_mla_krope, cache_diff_k, cache_diff_v, state_rwkv, state_rwkv_shift, state_gdn, state_gdn_conv, page_table, rel_bias, norm1_g, norm2_g, w_in, rw_mu, rw_w0, rw_w_up, rw_a0, rw_a_up, rw_g_up, rw_k_k, rw_k_a, rw_r_k, rw_ln_g, rw_ln_b, mla_q_norm_g, mla_w_uq, mla_kv_norm_g, mla_w_uk, mla_w_uv, mla_qn_g, mla_kn_g, gdn_conv_w, gdn_a_log, gdn_dt_bias, gdn_norm_g, dif_qn_g, dif_kn_g, dif_lam, dif_subln_g, w_branch, w_out, ffn_w_gate, ffn_w_up, ffn_w_down, moe_router, moe_w_gate, moe_w_up, moe_w_down):
    layer_params = dict(
        rw_mu=rw_mu, rw_w0=rw_w0, rw_w_up=rw_w_up, rw_a0=rw_a0, rw_a_up=rw_a_up, rw_g_up=rw_g_up,
        rw_k_k=rw_k_k, rw_k_a=rw_k_a, rw_r_k=rw_r_k, rw_ln_g=rw_ln_g, rw_ln_b=rw_ln_b,
        mla_q_norm_g=mla_q_norm_g, mla_w_uq=mla_w_uq, mla_kv_norm_g=mla_kv_norm_g, mla_w_uk=mla_w_uk,
        mla_w_uv=mla_w_uv, mla_qn_g=mla_qn_g, mla_kn_g=mla_kn_g, gdn_conv_w=gdn_conv_w, gdn_a_log=gdn_a_log,
        gdn_dt_bias=gdn_dt_bias, gdn_norm_g=gdn_norm_g, dif_qn_g=dif_qn_g, dif_kn_g=dif_kn_g, dif_lam=dif_lam,
        dif_subln_g=dif_subln_g)
    bp, sp, _ = x_prompt.shape
    bs, ls, _ = x_sample.shape
    tp, ts = bp * sp, bs * ls
    depth = w_in.shape[0]
    x = jnp.concatenate([x_prompt.reshape(tp, D_MODEL), x_sample.reshape(ts, D_MODEL)], axis=0)
    pos_p = jnp.arange(sp)
    pos_s = page_table.shape[1] * PAGE_SIZE + jnp.arange(ls)
    caches = (cache_mla_ckv, cache_mla_krope, cache_diff_k, cache_diff_v)
    sample_state = dict(rwkv=state_rwkv, rwkv_shift=state_rwkv_shift, gdn=state_gdn, gdn_conv=state_gdn_conv)
    got_p = {name: [] for name in STATE_KEYS}
    got_s = {name: [] for name in STATE_KEYS}
    for l in range(depth):
        lp = {name: arr[l] for name, arr in layer_params.items()}
        z = _rms_matmul(x, norm1_g[l], _pack_w_in(w_in[l]), tm=528, tn=2432, name="in_proj")
        st_p = dict(rwkv=jnp.zeros((bp, HEADS, HEAD_W, HEAD_W), F32), rwkv_shift=jnp.zeros((bp, RW_SHIFT_W), F32),
                    gdn=jnp.zeros((bp, HEADS, HEAD_W, HEAD_W), F32),
                    gdn_conv=jnp.zeros((bp, GDN_CONV - 1, GDN_QKV_W), F32))
        st_s = {name: arr[l] for name, arr in sample_state.items()}
        br_p, new_p = _group_branches(z[:tp].reshape(bp, sp, Z_W), pos_p, l, lp, rel_bias, st_p, None)
        br_s, new_s = _group_branches(z[tp:].reshape(bs, ls, Z_W), pos_s, l, lp, rel_bias, st_s, (page_table, caches))
        x = _merge_out(x, z, jnp.concatenate([br_p, br_s], axis=0), w_branch[l].astype(BF16), w_out[l].astype(BF16))
        if l % 2 == 0:
            combine = jnp.ones((1, tp + ts, 1), F32)
            x = _ffn(x, norm2_g[l], combine, ffn_w_gate[l // 2][None].astype(BF16), ffn_w_up[l // 2][None].astype(BF16),
                     ffn_w_down[l // 2][None].astype(BF16), tm=1056, tf=1408)
        else:
            router = jnp.pad(moe_router[l // 2], ((0, 0), (0, LANES - N_EXPERTS)))
            logits = _rms_matmul(x, norm2_g[l], router, tm=528, precision=HIGHEST, name="router")[:, :N_EXPERTS]
            top_v, top_i = lax.top_k(logits, TOP_K)
            top_w = jax.nn.softmax(top_v, axis=-1)
            combine = jnp.sum(jax.nn.one_hot(top_i, N_EXPERTS, dtype=F32) * top_w[..., None], axis=-2)
            combine = jnp.transpose(combine)[:, :, None]
            x = _ffn(x, norm2_g[l], combine, moe_w_gate[l // 2].astype(BF16), moe_w_up[l // 2].astype(BF16),
                     moe_w_down[l // 2].astype(BF16), tm=1056, tf=896)
        for name in STATE_KEYS:
            got_p[name].append(new_p[name])
            got_s[name].append(new_s[name])
    sp_out = {name: jnp.stack(v, axis=0) for name, v in got_p.items()}
    ss_out = {name: jnp.stack(v, axis=0) for name, v in got_s.items()}
    return ((x[:tp].reshape(bp, sp, D_MODEL), x[tp:].reshape(bs, ls, D_MODEL))
            + tuple(sp_out[name] for name in STATE_KEYS) + tuple(ss_out[name] for name in STATE_KEYS))
```

```python
import functools
import math

import jax
import jax.numpy as jnp
from jax import lax
from jax.experimental import pallas as pl
from jax.experimental.pallas import tpu as pltpu

F32 = jnp.float32
BF16 = jnp.bfloat16
HIGHEST = lax.Precision.HIGHEST
NEG = -1e30

D_MODEL = 1024
DEPTH = 2
PAST_LEN = 16384
PAGE_SIZE = 128
EPS = 1e-6
N_BRANCH = 4
BRANCH_W = 256
HEADS = 4
HEAD_W = 64
RW_DECAY_LORA, RW_AAA_LORA, RW_GATE_LORA = 32, 32, 64
RW_SHIFT_W = 3 * BRANCH_W + RW_DECAY_LORA + RW_AAA_LORA + RW_GATE_LORA
RW_LN_EPS = 64e-5
MLA_NOPE, MLA_ROPE = 64, 32
MLA_QK = MLA_NOPE + MLA_ROPE
MLA_Q_LORA, MLA_KV_LORA = 256, 128
ROPE_BASE = 10000.0
GDN_CONV = 4
GDN_CHUNK = 64
RW_CHUNK = 64
GDN_QKV_W = 768
DIFF_D = 32
DIFF_QKV_W = 768
REL_BUCKETS, REL_MAX_DIST = 32, 128
N_EXPERTS, TOP_K = 8, 2
PROJ_SIZES = (N_BRANCH * D_MODEL, RW_SHIFT_W, MLA_Q_LORA, MLA_KV_LORA, MLA_ROPE,
              GDN_QKV_W, HEADS, HEADS, BRANCH_W, DIFF_QKV_W)
Z_GATE, Z_RW, Z_MQ, Z_MKV, Z_QKV, Z_ZZ, Z_DIF, Z_SMALL = 0, 4096, 4992, 5248, 5376, 6144, 6400, 7168
Z_W = 7296

VMEM_LIMIT = 56 * 1024 * 1024
LANES = 128
SUBLANES = 8


def _params(sem, vmem=VMEM_LIMIT):
    return pltpu.CompilerParams(dimension_semantics=sem, vmem_limit_bytes=vmem)


def _row_tile(m, target):
    if m <= target:
        return m
    best = None
    for t in range(SUBLANES, target + 1, SUBLANES):
        if m % t == 0:
            best = t
    assert best is not None, (m, target)
    return best


def _mm_kernel(x_ref, w_ref, o_ref, *, precision):
    x, w = x_ref[...], w_ref[...]
    if precision is None:
        x, w = x.astype(BF16), w.astype(BF16)
    o_ref[...] = jnp.dot(x, w, preferred_element_type=F32, precision=precision)


def _matmul(x, w, *, tm=512, tn=None, precision=None, name="matmul"):
    m, k = x.shape
    n = w.shape[1]
    tm = _row_tile(m, tm)
    tn = tn or n
    return pl.pallas_call(
        functools.partial(_mm_kernel, precision=precision),
        out_shape=jax.ShapeDtypeStruct((m, n), F32),
        grid=(n // tn, m // tm),
        in_specs=[pl.BlockSpec((tm, k), lambda j, i: (i, 0)),
                  pl.BlockSpec((k, tn), lambda j, i: (0, j))],
        out_specs=pl.BlockSpec((tm, tn), lambda j, i: (i, j)),
        compiler_params=_params(("parallel", "parallel")),
        name=name,
    )(x, w)


def _rms_rows(x, g):
    return x * lax.rsqrt(jnp.mean(x * x, axis=-1, keepdims=True) + EPS) * g


def _rms_mm_kernel(x_ref, g_ref, w_ref, o_ref, *, precision):
    h = _rms_rows(x_ref[...], g_ref[...])
    w = w_ref[...]
    if precision is None:
        h, w = h.astype(BF16), w.astype(BF16)
    o_ref[...] = jnp.dot(h, w, preferred_element_type=F32, precision=precision)


def _rms_matmul(x, g, w, *, tm, tn=None, precision=None, name="rms_matmul"):
    m, k = x.shape
    n = w.shape[1]
    tm = _row_tile(m, tm)
    tn = tn or n
    return pl.pallas_call(
        functools.partial(_rms_mm_kernel, precision=precision),
        out_shape=jax.ShapeDtypeStruct((m, n), F32),
        grid=(n // tn, m // tm),
        in_specs=[pl.BlockSpec((tm, k), lambda j, i: (i, 0)),
                  pl.BlockSpec((1, k), lambda j, i: (0, 0)),
                  pl.BlockSpec((k, tn), lambda j, i: (0, j))],
        out_specs=pl.BlockSpec((tm, tn), lambda j, i: (i, j)),
        compiler_params=_params(("parallel", "parallel")),
        name=name,
    )(x, g.reshape(1, k), w)


def _merge_kernel(x_ref, zg_ref, br_ref, wb_ref, wo_ref, o_ref):
    br = br_ref[...].astype(BF16)
    merged = None
    for n in range(N_BRANCH):
        gate = jax.nn.sigmoid(zg_ref[:, n * D_MODEL:(n + 1) * D_MODEL])
        p = jnp.dot(br[:, n * BRANCH_W:(n + 1) * BRANCH_W], wb_ref[n], preferred_element_type=F32)
        merged = gate * p if merged is None else merged + gate * p
    o_ref[...] = x_ref[...] + jnp.dot(merged.astype(BF16), wo_ref[...], preferred_element_type=F32)


def _merge_out(x, z, br, wb, wo, *, tm=528):
    m = x.shape[0]
    tm = _row_tile(m, tm)
    return pl.pallas_call(
        _merge_kernel,
        out_shape=jax.ShapeDtypeStruct((m, D_MODEL), F32),
        grid=(m // tm,),
        in_specs=[pl.BlockSpec((tm, D_MODEL), lambda i: (i, 0)),
                  pl.BlockSpec((tm, N_BRANCH * D_MODEL), lambda i: (i, 0)),
                  pl.BlockSpec((tm, N_BRANCH * BRANCH_W), lambda i: (i, 0)),
                  pl.BlockSpec((N_BRANCH, BRANCH_W, D_MODEL), lambda i: (0, 0, 0)),
                  pl.BlockSpec((D_MODEL, D_MODEL), lambda i: (0, 0))],
        out_specs=pl.BlockSpec((tm, D_MODEL), lambda i: (i, 0)),
        compiler_params=_params(("parallel",)),
        name="merge_out",
    )(x, z, br, wb, wo)


def _ffn_kernel(x_ref, g_ref, c_ref, wg_ref, wu_ref, wd_ref, o_ref, h_sc, acc_sc):
    e, j = pl.program_id(1), pl.program_id(2)

    @pl.when((e == 0) & (j == 0))
    def _():
        h_sc[...] = _rms_rows(x_ref[...], g_ref[...]).astype(BF16)
        acc_sc[...] = jnp.zeros_like(acc_sc)

    h = h_sc[...]
    a = jnp.dot(h, wg_ref[0], preferred_element_type=F32)
    u = jnp.dot(h, wu_ref[0], preferred_element_type=F32)
    act = (jax.nn.silu(a) * u).astype(BF16)
    acc_sc[...] += c_ref[0] * jnp.dot(act, wd_ref[0], preferred_element_type=F32)

    @pl.when((e == pl.num_programs(1) - 1) & (j == pl.num_programs(2) - 1))
    def _():
        o_ref[...] = x_ref[...] + acc_sc[...]


def _ffn(x, g, combine, wg, wu, wd, *, tm, tf):
    m = x.shape[0]
    n_e, _, f = wg.shape
    tm = _row_tile(m, tm)
    return pl.pallas_call(
        _ffn_kernel,
        out_shape=jax.ShapeDtypeStruct((m, D_MODEL), F32),
        grid=(m // tm, n_e, f // tf),
        in_specs=[pl.BlockSpec((tm, D_MODEL), lambda i, e, j: (i, 0)),
                  pl.BlockSpec((1, D_MODEL), lambda i, e, j: (0, 0)),
                  pl.BlockSpec((1, tm, 1), lambda i, e, j: (e, i, 0)),
                  pl.BlockSpec((1, D_MODEL, tf), lambda i, e, j: (e, 0, j)),
                  pl.BlockSpec((1, D_MODEL, tf), lambda i, e, j: (e, 0, j)),
                  pl.BlockSpec((1, tf, D_MODEL), lambda i, e, j: (e, j, 0))],
        out_specs=pl.BlockSpec((tm, D_MODEL), lambda i, e, j: (i, 0)),
        scratch_shapes=[pltpu.VMEM((tm, D_MODEL), BF16), pltpu.VMEM((tm, D_MODEL), F32)],
        compiler_params=_params(("parallel", "arbitrary", "arbitrary")),
        name="ffn",
    )(x, g.reshape(1, D_MODEL), combine, wg, wu, wd)


def _flash_kernel(q_ref, k_ref, v_ref, b_ref, o_ref, m_sc, l_sc, acc_sc, *, t):
    qi = pl.program_id(2)
    q = q_ref[0, 0]
    m_sc[...] = jnp.full_like(m_sc, NEG)
    l_sc[...] = jnp.zeros_like(l_sc)
    acc_sc[...] = jnp.zeros_like(acc_sc)

    def body(kj, carry):
        off = pl.multiple_of(kj * t, t)
        k = k_ref[0, 0, pl.ds(off, t), :]
        v = v_ref[0, 0, pl.ds(off, t), :]
        s = lax.dot_general(q, k, (((1,), (1,)), ((), ())), preferred_element_type=F32)
        s = s + b_ref[0, jnp.minimum(qi - kj, 2)]
        m_prev = m_sc[...]
        m_new = jnp.maximum(m_prev, jnp.max(s, axis=-1, keepdims=True))
        alpha = jnp.exp(m_prev - m_new)
        p = jnp.exp(s - m_new)
        l_sc[...] = alpha * l_sc[...] + jnp.sum(p, axis=-1, keepdims=True)
        acc_sc[...] = alpha * acc_sc[...] + jnp.dot(p.astype(BF16), v, preferred_element_type=F32)
        m_sc[...] = m_new
        return carry

    lax.fori_loop(0, qi + 1, body, 0)
    o_ref[0, 0] = acc_sc[...] / l_sc[...]


def _flash(q, k, v, bias, *, t):
    b, n_map, s, d = q.shape
    hv, dv = v.shape[1], v.shape[3]
    rep = n_map // hv
    bias_idx = (lambda bi, h, i: (h // rep, 0, 0, 0)) if bias.shape[0] > 1 else (lambda bi, h, i: (0, 0, 0, 0))
    return pl.pallas_call(
        functools.partial(_flash_kernel, t=t),
        out_shape=jax.ShapeDtypeStruct((b, n_map, s, dv), F32),
        grid=(b, n_map, s // t),
        in_specs=[pl.BlockSpec((1, 1, t, d), lambda bi, h, i: (bi, h, i, 0)),
                  pl.BlockSpec((1, 1, s, d), lambda bi, h, i: (bi, h, 0, 0)),
                  pl.BlockSpec((1, 1, s, dv), lambda bi, h, i: (bi, h // rep, 0, 0)),
                  pl.BlockSpec((1, 3, t, t), bias_idx)],
        out_specs=pl.BlockSpec((1, 1, t, dv), lambda bi, h, i: (bi, h, i, 0)),
        scratch_shapes=[pltpu.VMEM((t, 1), F32), pltpu.VMEM((t, 1), F32), pltpu.VMEM((t, dv), F32)],
        compiler_params=_params(("parallel", "parallel", "arbitrary")),
        name="flash",
    )(q, k, v, bias)


_MM = (((1,), (0,)), ((), ()))
_MM_BT = (((1,), (1,)), ((), ()))
_MM_AT = (((0,), (0,)), ((), ()))
SOLVE_BASE = 8


def _split_bf16(a):
    hi = a.astype(BF16)
    return hi, (a - hi.astype(F32)).astype(BF16)


def _dot3(a, b, dims=_MM):
    ah, al = _split_bf16(a)
    bh, bl = _split_bf16(b)
    dg = lambda x, y: lax.dot_general(x, y, dims, preferred_element_type=F32)
    return dg(ah, bh) + (dg(ah, bl) + dg(al, bh))


def _dot_mask(mask, x, mask_first=True):
    m = mask.astype(BF16)
    hi = x.astype(BF16)
    rest = x - hi.astype(F32)
    mid = rest.astype(BF16)
    lo = (rest - mid.astype(F32)).astype(BF16)
    if mask_first:
        dg = lambda y: lax.dot_general(m, y, _MM, preferred_element_type=F32)
    else:
        dg = lambda y: lax.dot_general(y, m, _MM, preferred_element_type=F32)
    return dg(hi) + (dg(mid) + dg(lo))


def _block_ids(c):
    ii = lax.broadcasted_iota(jnp.int32, (c, c), 0)
    jj = lax.broadcasted_iota(jnp.int32, (c, c), 1)
    return ii, jj


def _same_block(ii, jj, size):
    shift = jnp.full(ii.shape, size.bit_length() - 1, ii.dtype)
    return lax.shift_right_logical(ii, shift) == lax.shift_right_logical(jj, shift)


def _unit_lower_inverse(low, c, ii, jj):
    eye = (ii == jj).astype(F32)
    d = jnp.where(_same_block(ii, jj, SOLVE_BASE), low, 0.0)
    d2 = _dot3(d, d)
    d4 = _dot3(d2, d2)
    inv = _dot3(_dot3(eye - d, eye + d2), eye + d4)
    size = SOLVE_BASE
    while size < c:
        e = jnp.where(_same_block(ii, jj, 2 * size), jnp.where(_same_block(ii, jj, size), 0.0, low), 0.0)
        inv = inv - _dot3(_dot3(inv, e), inv)
        size *= 2
    return inv


def _gdn_kernel(q_ref, k_ref, v_ref, gcol_ref, grow_ref, bcol_ref, s0_ref, o_ref, sout_ref, s_sc, *, bb, c):
    @pl.when(pl.program_id(1) == 0)
    def _():
        s_sc[...] = s0_ref[...]

    ii, jj = _block_ids(c)
    causal = ii >= jj
    for b in range(bb):
        for h in range(HEADS):
            q = q_ref[b, h] * (HEAD_W ** -0.5)
            k, v = k_ref[b, h], v_ref[b, h]
            beta = bcol_ref[b, h, 0]
            gc_c = _dot_mask(causal, jnp.broadcast_to(gcol_ref[b, h, 0], (c, c)))
            gc_r = _dot_mask(ii <= jj, jnp.broadcast_to(grow_ref[b, h, 0], (c, c)), mask_first=False)
            decay = jnp.where(causal, jnp.exp(jnp.where(causal, gc_c - gc_r, 0.0)), 0.0)
            gc = gc_c[:, 0:1]
            gc_last = gc_c[c - 1:c, 0:1]
            kb = k * beta
            a_low = jnp.where(ii > jj, _dot3(kb, k, _MM_BT) * decay, 0.0)
            x = _dot3(_unit_lower_inverse(a_low, c, ii, jj), jnp.concatenate([v * beta, kb * jnp.exp(gc)], axis=1))
            u, w = x[:, :HEAD_W], x[:, HEAD_W:]
            s = s_sc[b, h]
            v_new = u - _dot3(w, s)
            qk = _dot3(q, k, _MM_BT) * decay
            o_ref[b, h] = _dot3(q * jnp.exp(gc), s) + _dot3(qk, v_new)
            kg = k * jnp.exp(gc_last - gc)
            s_sc[b, h] = s * jnp.exp(gc_last) + _dot3(kg, v_new, _MM_AT)
    sout_ref[...] = s_sc[...]


def _gdn_chunks(q, k, v, g, beta, s0, *, c, bb):
    b, h, l, _ = q.shape
    n = l // c
    gcol = g.reshape(b, h, n, c, 1)
    grow = g.reshape(b, h, n, 1, c)
    bcol = beta.reshape(b, h, n, c, 1)
    seq = pl.BlockSpec((bb, h, c, HEAD_W), lambda i, j: (i, 0, j, 0))
    col = pl.BlockSpec((bb, h, 1, c, 1), lambda i, j: (i, 0, j, 0, 0))
    row = pl.BlockSpec((bb, h, 1, 1, c), lambda i, j: (i, 0, j, 0, 0))
    st = pl.BlockSpec((bb, h, HEAD_W, HEAD_W), lambda i, j: (i, 0, 0, 0))
    return pl.pallas_call(
        functools.partial(_gdn_kernel, bb=bb, c=c),
        out_shape=(jax.ShapeDtypeStruct((b, h, l, HEAD_W), F32),
                   jax.ShapeDtypeStruct((b, h, HEAD_W, HEAD_W), F32)),
        grid=(b // bb, n),
        in_specs=[seq, seq, seq, col, row, col, st],
        out_specs=(seq, st),
        scratch_shapes=[pltpu.VMEM((bb, h, HEAD_W, HEAD_W), F32)],
        compiler_params=_params(("parallel", "arbitrary")),
        name="gdn_chunks",
    )(q, k, v, gcol, grow, bcol, s0)


def _rwkv_kernel(r_ref, w_ref, k_ref, v_ref, kk_ref, kka_ref, s0_ref, y_ref, sout_ref, s_sc, *, bb, tc, grp):
    @pl.when(pl.program_id(1) == 0)
    def _():
        s_sc[...] = s0_ref[...]

    ones_blk = (lax.broadcasted_iota(jnp.int32, (BRANCH_W, BRANCH_W), 0) // HEAD_W ==
                lax.broadcasted_iota(jnp.int32, (BRANCH_W, BRANCH_W), 1) // HEAD_W).astype(F32)
    dmask = (lax.broadcasted_iota(jnp.int32, (HEAD_W, BRANCH_W), 0) ==
             lax.broadcasted_iota(jnp.int32, (HEAD_W, BRANCH_W), 1) % HEAD_W).astype(F32)
    seg = functools.partial(jnp.dot, preferred_element_type=F32, precision=HIGHEST)

    def group(gi, carry):
        base = gi * grp
        for t in range(grp):
            for b in range(bb):
                row = lambda ref: ref[b, pl.ds(base + t, 1), :]
                s = s_sc[b]
                v_b = seg(dmask * row(v_ref), ones_blk)
                s_kk = seg(s * row(kk_ref), ones_blk)
                s = s * row(w_ref) - s_kk * row(kka_ref) + v_b * row(k_ref)
                y_b = seg(s * row(r_ref), ones_blk)
                y_ref[b, pl.ds(base + t, 1), :] = jnp.sum(dmask * y_b, axis=0, keepdims=True)
                s_sc[b] = s
        return carry

    lax.fori_loop(0, tc // grp, group, 0)
    sout_ref[...] = s_sc[...]


def _rwkv_scan(r, w, k, v, kk, kka, s0, *, bb, tc):
    b, l, _ = r.shape
    grp = math.gcd(tc, 8)
    seq = pl.BlockSpec((bb, tc, BRANCH_W), lambda i, j: (i, j, 0))
    st = pl.BlockSpec((bb, HEAD_W, BRANCH_W), lambda i, j: (i, 0, 0))
    return pl.pallas_call(
        functools.partial(_rwkv_kernel, bb=bb, tc=tc, grp=grp),
        out_shape=(jax.ShapeDtypeStruct((b, l, BRANCH_W), F32),
                   jax.ShapeDtypeStruct((b, HEAD_W, BRANCH_W), F32)),
        grid=(b // bb, l // tc),
        in_specs=[seq] * 6 + [st],
        out_specs=(seq, st),
        scratch_shapes=[pltpu.VMEM((bb, HEAD_W, BRANCH_W), F32)],
        compiler_params=_params(("parallel", "arbitrary")),
        name="rwkv_scan",
    )(r, w, k, v, kk, kka, s0)


def _rwkv_chunk_kernel(r_ref, lw_ref, k_ref, v_ref, kk_ref, kka_ref, s0_ref, y_ref, sout_ref, s_sc, *, bb, c):
    @pl.when(pl.program_id(1) == 0)
    def _():
        s_sc[...] = s0_ref[...]

    ii, jj = _block_ids(c)
    incl, strict = ii >= jj, ii > jj
    for b in range(bb):
        for h in range(HEADS):
            r, lw, k, v = r_ref[b, h], lw_ref[b, h], k_ref[b, h], v_ref[b, h]
            kap, bq = kk_ref[b, h], kka_ref[b, h]
            lam = _dot_mask(incl, lw)
            lam_end = lam[c - 1:c, :]
            e_neg = jnp.exp(-lam)
            kap_t = kap * jnp.exp(lam - lw)
            b_t, k_t, r_t = bq * e_neg, k * e_neg, r * jnp.exp(lam)
            e_end = jnp.exp(lam_end - lam)
            l_b = jnp.where(strict, _dot3(kap_t, b_t, _MM_BT), 0.0)
            l_k = jnp.where(strict, _dot3(kap_t, k_t, _MM_BT), 0.0)
            a_rk = jnp.where(incl, _dot3(r_t, k_t, _MM_BT), 0.0)
            a_rb = jnp.where(incl, _dot3(r_t, b_t, _MM_BT), 0.0)
            t_inv = _unit_lower_inverse(l_b, c, ii, jj)
            p1 = _dot3(t_inv, kap_t)
            p2 = _dot3(t_inv, _dot3(l_k, v))
            s = s_sc[b, h]
            u = _dot3(p1, s, _MM_BT) + p2
            y_ref[b, h] = _dot3(r_t, s, _MM_BT) + _dot3(a_rk, v) - _dot3(a_rb, u)
            s_sc[b, h] = (s * jnp.exp(lam_end) + _dot3(v, k * e_end, _MM_AT) - _dot3(u, bq * e_end, _MM_AT))
    sout_ref[...] = s_sc[...]


def _rwkv_chunks(r, lw, k, v, kk, kka, s0, *, c, bb):
    b, h, l, _ = r.shape
    seq = pl.BlockSpec((bb, h, c, HEAD_W), lambda i, j: (i, 0, j, 0))
    st = pl.BlockSpec((bb, h, HEAD_W, HEAD_W), lambda i, j: (i, 0, 0, 0))
    return pl.pallas_call(
        functools.partial(_rwkv_chunk_kernel, bb=bb, c=c),
        out_shape=(jax.ShapeDtypeStruct((b, h, l, HEAD_W), F32),
                   jax.ShapeDtypeStruct((b, h, HEAD_W, HEAD_W), F32)),
        grid=(b // bb, l // c),
        in_specs=[seq] * 6 + [st],
        out_specs=(seq, st),
        scratch_shapes=[pltpu.VMEM((bb, h, HEAD_W, HEAD_W), F32)],
        compiler_params=_params(("parallel", "arbitrary")),
        name="rwkv_chunks",
    )(r, lw, k, v, kk, kka, s0)


def _page_copy(pool, buf, sem, layer, page, slot, p, tokens_on_lanes):
    if tokens_on_lanes:
        dst = buf.at[slot, :, pl.ds(p * PAGE_SIZE, PAGE_SIZE)]
    else:
        dst = buf.at[slot, pl.ds(p * PAGE_SIZE, PAGE_SIZE), :]
    return pltpu.make_async_copy(pool.at[layer, page], dst, sem.at[slot])


def _paged_stream(pt_ref, pools, bufs, sems, lanes, *, layer, pc, n_chunks, process):
    b = pl.program_id(0)
    assert n_chunks % 2 == 0

    def fetch(seq, chunk, slot):
        for p in range(pc):
            page = pt_ref[seq, chunk * pc + p]
            for pool, buf, sem, on_lanes in zip(pools, bufs, sems, lanes):
                _page_copy(pool, buf, sem, layer, page, slot, p, on_lanes).start()

    def wait(slot):
        for p in range(pc):
            for pool, buf, sem, on_lanes in zip(pools, bufs, sems, lanes):
                _page_copy(pool, buf, sem, layer, 0, slot, p, on_lanes).wait()

    @pl.when(b == 0)
    def _():
        fetch(0, 0, 0)

    for ch in range(n_chunks):
        slot = ch % 2
        if ch + 1 < n_chunks:
            fetch(b, ch + 1, 1 - slot)
        else:
            @pl.when(b + 1 < pl.num_programs(0))
            def _():
                fetch(b + 1, 0, 1 - slot)
        wait(slot)
        process(ch, slot)


def _online_softmax_step(s, values, m_sc, l_sc, acc_sc, values_transposed):
    m_prev = m_sc[...]
    m_new = jnp.maximum(m_prev, jnp.max(s, axis=-1, keepdims=True))
    alpha = jnp.exp(m_prev - m_new)
    p = jnp.exp(s - m_new)
    l_sc[...] = alpha * l_sc[...] + jnp.sum(p, axis=-1, keepdims=True)
    pv = lax.dot_general(p.astype(BF16), values, _MM_BT if values_transposed else _MM, preferred_element_type=F32)
    acc_sc[...] = alpha * acc_sc[...] + pv
    m_sc[...] = m_new


def _mla_dec_kernel(pt_ref, qa_ref, qr_ref, wt_ref, wuv_ref, cn_ref, rn_ref, mn_ref, ckv_hbm, kr_hbm, o_ref,
                    cbuf, rbuf, sem_c, sem_r, m_sc, l_sc, acc_sc, *, layer, pc, n_chunks):
    def process(c, r_t, mask):
        cb = c.astype(BF16)
        knt = lax.dot_general(wt_ref[...], cb, _MM_BT, preferred_element_type=F32)
        kn2 = knt * knt
        ssr = jnp.sum(r_t * r_t, axis=0, keepdims=True)
        s_raw = (lax.dot_general(qa_ref[0], cb, _MM_BT, preferred_element_type=F32)
                 + jnp.dot(qr_ref[0], r_t.astype(BF16), preferred_element_type=F32))
        parts = []
        for h in range(HEADS):
            ss = jnp.sum(kn2[h * HEAD_W:(h + 1) * HEAD_W], axis=0, keepdims=True) + ssr
            rinv = lax.rsqrt(ss * (1.0 / MLA_QK) + EPS)
            parts.append(s_raw[h * SUBLANES:(h + 1) * SUBLANES] * rinv)
        s = jnp.concatenate(parts, axis=0)
        if mask is not None:
            s = s + mask
        _online_softmax_step(s, cb, m_sc, l_sc, acc_sc, values_transposed=False)

    m_sc[...] = jnp.full_like(m_sc, NEG)
    l_sc[...] = jnp.zeros_like(l_sc)
    acc_sc[...] = jnp.zeros_like(acc_sc)
    _paged_stream(pt_ref, [ckv_hbm, kr_hbm], [cbuf, rbuf], [sem_c, sem_r], [False, True],
                  layer=layer, pc=pc, n_chunks=n_chunks,
                  process=lambda ch, slot: process(cbuf[slot], rbuf[slot], None))
    process(cn_ref[0], rn_ref[0], mn_ref[...])
    ctx = (acc_sc[...] / l_sc[...]).astype(BF16)
    for h in range(HEADS):
        o_ref[0, h * SUBLANES:(h + 1) * SUBLANES, :] = jnp.dot(
            ctx[h * SUBLANES:(h + 1) * SUBLANES], wuv_ref[h], preferred_element_type=F32)


def _mla_decode(page_table, qa, qr, wt, wuv, c_new, r_new, mask_new, ckv_pool, kr_pool, *, layer, pc=16):
    b, n_pages = page_table.shape
    rows = HEADS * SUBLANES
    n_chunks = n_pages // pc
    n = pc * PAGE_SIZE
    per_seq = lambda shape: pl.BlockSpec((1,) + shape, lambda i, pt: (i, 0, 0))
    whole = lambda shape: pl.BlockSpec(shape, lambda i, pt: (0,) * len(shape))
    return pl.pallas_call(
        functools.partial(_mla_dec_kernel, layer=layer, pc=pc, n_chunks=n_chunks),
        out_shape=jax.ShapeDtypeStruct((b, rows, HEAD_W), F32),
        grid_spec=pltpu.PrefetchScalarGridSpec(
            num_scalar_prefetch=1,
            grid=(b,),
            in_specs=[per_seq((rows, MLA_KV_LORA)), per_seq((rows, MLA_ROPE)),
                      whole((BRANCH_W, MLA_KV_LORA)), whole((HEADS, MLA_KV_LORA, HEAD_W)),
                      per_seq((SUBLANES, MLA_KV_LORA)), per_seq((MLA_ROPE, SUBLANES)),
                      whole((rows, SUBLANES)),
                      pl.BlockSpec(memory_space=pl.ANY), pl.BlockSpec(memory_space=pl.ANY)],
            out_specs=per_seq((rows, HEAD_W)),
            scratch_shapes=[pltpu.VMEM((2, n, MLA_KV_LORA), F32), pltpu.VMEM((2, MLA_ROPE, n), F32),
                            pltpu.SemaphoreType.DMA((2,)), pltpu.SemaphoreType.DMA((2,)),
                            pltpu.VMEM((rows, 1), F32), pltpu.VMEM((rows, 1), F32),
                            pltpu.VMEM((rows, MLA_KV_LORA), F32)]),
        compiler_params=_params(("arbitrary",)),
        name="mla_decode",
    )(page_table, qa, qr, wt, wuv, c_new, r_new, mask_new, ckv_pool, kr_pool)


def _dif_dec_kernel(pt_ref, q_ref, kn_ref, vn_ref, bfar_ref, blast_ref, bnew_ref, dk_hbm, dv_hbm, o_ref,
                    kbuf, vbuf, sem_k, sem_v, m_sc, l_sc, acc_sc, *, layer, pc, n_chunks):
    def process(k_t, v_t, bias):
        s = jnp.dot(q_ref[0], k_t.astype(BF16), preferred_element_type=F32) + bias
        _online_softmax_step(s, v_t.astype(BF16), m_sc, l_sc, acc_sc, values_transposed=True)

    m_sc[...] = jnp.full_like(m_sc, NEG)
    l_sc[...] = jnp.zeros_like(l_sc)
    acc_sc[...] = jnp.zeros_like(acc_sc)
    _paged_stream(pt_ref, [dk_hbm, dv_hbm], [kbuf, vbuf], [sem_k, sem_v], [True, True],
                  layer=layer, pc=pc, n_chunks=n_chunks,
                  process=lambda ch, slot: process(kbuf[slot], vbuf[slot],
                                                   blast_ref[...] if ch == n_chunks - 1 else bfar_ref[...]))
    process(kn_ref[0], vn_ref[0], bnew_ref[...])
    o_ref[0] = acc_sc[...] / l_sc[...]


def _dif_decode(page_table, q_bd, k_new, v_new, bias_far, bias_last, bias_new, dk_pool, dv_pool, *, layer, pc=16):
    b, n_pages = page_table.shape
    rows = q_bd.shape[1]
    n_chunks = n_pages // pc
    n = pc * PAGE_SIZE
    per_seq = lambda shape: pl.BlockSpec((1,) + shape, lambda i, pt: (i, 0, 0))
    whole = lambda shape: pl.BlockSpec(shape, lambda i, pt: (0,) * len(shape))
    return pl.pallas_call(
        functools.partial(_dif_dec_kernel, layer=layer, pc=pc, n_chunks=n_chunks),
        out_shape=jax.ShapeDtypeStruct((b, rows, BRANCH_W), F32),
        grid_spec=pltpu.PrefetchScalarGridSpec(
            num_scalar_prefetch=1,
            grid=(b,),
            in_specs=[per_seq((rows, BRANCH_W)), per_seq((BRANCH_W, SUBLANES)), per_seq((BRANCH_W, SUBLANES)),
                      whole((rows, 1)), whole((rows, n)), whole((rows, SUBLANES)),
                      pl.BlockSpec(memory_space=pl.ANY), pl.BlockSpec(memory_space=pl.ANY)],
            out_specs=per_seq((rows, BRANCH_W)),
            scratch_shapes=[pltpu.VMEM((2, BRANCH_W, n), F32), pltpu.VMEM((2, BRANCH_W, n), F32),
                            pltpu.SemaphoreType.DMA((2,)), pltpu.SemaphoreType.DMA((2,)),
                            pltpu.VMEM((rows, 1), F32), pltpu.VMEM((rows, 1), F32),
                            pltpu.VMEM((rows, BRANCH_W), F32)]),
        compiler_params=_params(("arbitrary",)),
        name="dif_decode",
    )(page_table, q_bd, k_new, v_new, bias_far, bias_last, bias_new, dk_pool, dv_pool)


def _rms(x, g):
    return x * lax.rsqrt(jnp.mean(x * x, axis=-1, keepdims=True) + EPS) * g


def _l2(x):
    return x * lax.rsqrt(jnp.maximum(jnp.sum(x * x, axis=-1, keepdims=True), 1e-12))


def _rope(x, pos):
    half = x.shape[-1] // 2
    inv = ROPE_BASE ** (-jnp.arange(half, dtype=F32) / half)
    ang = pos.astype(F32)[:, None] * inv[None, :]
    cos, sin = jnp.cos(ang)[:, None, :], jnp.sin(ang)[:, None, :]
    x1, x2 = x[..., :half], x[..., half:]
    return jnp.concatenate([x1 * cos - x2 * sin, x1 * sin + x2 * cos], axis=-1)


def _rel_bias(rel_table, qpos, kpos):
    n = jnp.maximum(qpos[:, None] - kpos[None, :], 0)
    exact = REL_BUCKETS // 2
    nf = jnp.maximum(n, 1).astype(F32)
    large = exact + (jnp.log(nf / exact) / math.log(REL_MAX_DIST / exact) * (REL_BUCKETS - exact)).astype(jnp.int32)
    bucket = jnp.where(n < exact, n, jnp.minimum(large, REL_BUCKETS - 1))
    picked = jnp.einsum('qkb,bh->hqk', jax.nn.one_hot(bucket, REL_BUCKETS, dtype=F32), rel_table.astype(F32),
                        precision=HIGHEST)
    return picked


def _pad_rows(x, axis, to):
    pad = [(0, 0)] * x.ndim
    pad[axis] = (0, to - x.shape[axis])
    return jnp.pad(x, pad)


def _pack_w_in(w_in):
    o, parts = 0, []
    for wdt in PROJ_SIZES:
        parts.append(w_in[:, o:o + wdt])
        o += wdt
    zg, zrw, zmq, zmkv, zmkr, zqkv, zb, za, zz, zdif = parts
    small = jnp.concatenate([zmkr, zb, za], axis=1)
    small = jnp.pad(small, ((0, 0), (0, Z_W - Z_SMALL - small.shape[1])))
    return jnp.concatenate([zg, zrw, zmq, zmkv, zqkv, zz, zdif, small], axis=1).astype(BF16)


def _rwkv_branch(zr, shift_prev, s0, lp, *, bb, tc):
    b, l, _ = zr.shape
    prev = jnp.concatenate([shift_prev[:, None], zr[:, :-1]], axis=1)
    zs = (zr + (prev - zr) * lp['rw_mu']).reshape(b * l, RW_SHIFT_W)
    o1, o2, o3 = BRANCH_W, 2 * BRANCH_W, 3 * BRANCH_W
    r, k, v = zs[:, :o1], zs[:, o1:o2], zs[:, o2:o3]
    wd = zs[:, o3:o3 + RW_DECAY_LORA]
    ad = zs[:, o3 + RW_DECAY_LORA:o3 + RW_DECAY_LORA + RW_AAA_LORA]
    gd = zs[:, o3 + RW_DECAY_LORA + RW_AAA_LORA:]
    tm = 2048
    w_log = -jax.nn.softplus(-(lp['rw_w0'] + _matmul(jnp.tanh(wd), lp['rw_w_up'], tm=tm, precision=HIGHEST))) - 0.5
    log_decay = -jnp.exp(w_log)
    a = jax.nn.sigmoid(lp['rw_a0'] + _matmul(ad, lp['rw_a_up'], tm=tm, precision=HIGHEST))
    g = _matmul(jax.nn.sigmoid(gd), lp['rw_g_up'], tm=tm, precision=HIGHEST)
    heads = lambda t: t.reshape(b * l, HEADS, HEAD_W)
    kk = _l2(heads(k * lp['rw_k_k'])).reshape(b * l, BRANCH_W)
    k = k * (1.0 + (a - 1.0) * lp['rw_k_a'])
    if l % RW_CHUNK == 0:
        hm = lambda t: jnp.transpose(t.reshape(b, l, HEADS, HEAD_W), (0, 2, 1, 3))
        y, s_new = _rwkv_chunks(hm(r), hm(log_decay), hm(k), hm(v), hm(kk), hm(kk * a), s0, c=RW_CHUNK, bb=bb)
        y = jnp.transpose(y, (0, 2, 1, 3)).reshape(b * l, HEADS, HEAD_W)
    else:
        seq = lambda t: t.reshape(b, l, BRANCH_W)
        s0_t = jnp.transpose(s0, (0, 2, 1, 3)).reshape(b, HEAD_W, BRANCH_W)
        y, s_new = _rwkv_scan(seq(r), seq(jnp.exp(log_decay)), seq(k), seq(v), seq(kk), seq(kk * a), s0_t,
                              bb=bb, tc=tc)
        s_new = jnp.transpose(s_new.reshape(b, HEAD_W, HEADS, HEAD_W), (0, 2, 1, 3))
        y = heads(y.reshape(b * l, BRANCH_W))
    mu = jnp.mean(y, axis=-1, keepdims=True)
    var = jnp.mean(jnp.square(y - mu), axis=-1, keepdims=True)
    y = ((y - mu) * lax.rsqrt(var + RW_LN_EPS) * lp['rw_ln_g'].reshape(HEADS, HEAD_W)
         + lp['rw_ln_b'].reshape(HEADS, HEAD_W))
    y = y + jnp.sum(heads(r) * heads(k) * lp['rw_r_k'], axis=-1, keepdims=True) * heads(v)
    return (y.reshape(b * l, BRANCH_W) * g).reshape(b, l, BRANCH_W), s_new, zr[:, -1]


def _gdn_branch(zqkv, zb, za, zz, conv_prev, s0, lp, *, bb):
    b, l, _ = zqkv.shape
    xc = jnp.concatenate([conv_prev, zqkv], axis=1)
    wc = lp['gdn_conv_w']
    conv = sum(xc[:, j:j + l] * wc[j] for j in range(GDN_CONV))
    qkv = jax.nn.silu(conv)
    hm = lambda t: jnp.transpose(t.reshape(b, l, HEADS, HEAD_W), (0, 2, 1, 3))
    q = _l2(hm(qkv[..., :BRANCH_W]))
    k = _l2(hm(qkv[..., BRANCH_W:2 * BRANCH_W]))
    v = hm(qkv[..., 2 * BRANCH_W:])
    beta = jnp.transpose(jax.nn.sigmoid(zb), (0, 2, 1))
    g = jnp.transpose(-jnp.exp(lp['gdn_a_log']) * jax.nn.softplus(za + lp['gdn_dt_bias']), (0, 2, 1))
    c = math.gcd(l, GDN_CHUNK)
    lpad = l
    if c < SUBLANES:
        lpad = c = SUBLANES
        q, k, v = (_pad_rows(t, 2, lpad) for t in (q, k, v))
        g, beta = _pad_rows(g, 2, lpad), _pad_rows(beta, 2, lpad)
    o, s_new = _gdn_chunks(q, k, v, g, beta, s0, c=c, bb=bb)
    o = jnp.transpose(o[:, :, :l], (0, 2, 1, 3))
    o = _rms(o, lp['gdn_norm_g']) * jax.nn.silu(zz.reshape(b, l, HEADS, HEAD_W))
    return o.reshape(b, l, BRANCH_W), s_new, xc[:, -(GDN_CONV - 1):]


def _mla_project(zmq, zmkv, zmkr, pos, lp):
    b, l, _ = zmq.shape
    cq = _rms(zmq, lp['mla_q_norm_g']).reshape(b * l, MLA_Q_LORA)
    q = _matmul(cq, lp['mla_w_uq'].reshape(MLA_Q_LORA, HEADS * MLA_QK), tm=2048).reshape(b, l, HEADS, MLA_QK)
    q = jnp.concatenate([q[..., :MLA_NOPE], _rope(q[..., MLA_NOPE:], pos)], axis=-1)
    q = _rms(q, lp['mla_qn_g'])
    ckv = _rms(zmkv, lp['mla_kv_norm_g'])
    kr = _rope(zmkr[:, :, None, :], pos)[:, :, 0]
    return q, ckv, kr


def _diff_project(zd, lp):
    b, l, _ = zd.shape
    w = HEADS * 2 * DIFF_D
    q = _rms(zd[..., :w].reshape(b, l, HEADS, 2, DIFF_D), lp['dif_qn_g'])
    k = _rms(zd[..., w:2 * w].reshape(b, l, HEADS, 2, DIFF_D), lp['dif_kn_g'])
    return (q.reshape(b, l, HEADS, 2 * DIFF_D), k.reshape(b, l, HEADS, 2 * DIFF_D),
            zd[..., 2 * w:].reshape(b, l, HEADS, HEAD_W))


def _diff_lambda(lam, l):
    lam_init = 0.8 - 0.6 * math.exp(-0.3 * l)
    return jnp.exp(jnp.sum(lam[0] * lam[1])) - jnp.exp(jnp.sum(lam[2] * lam[3])) + lam_init, lam_init


def _prompt_attention(q_m, ckv, kr, q_d, k_d, v_d, lam, lp, rel_table, *, t):
    b, s = q_m.shape[:2]
    flat = ckv.reshape(b * s, MLA_KV_LORA)
    k_nope = _matmul(flat, lp['mla_w_uk'].reshape(MLA_KV_LORA, BRANCH_W), tm=2048).reshape(b, s, HEADS, MLA_NOPE)
    v_m = _matmul(flat, lp['mla_w_uv'].reshape(MLA_KV_LORA, BRANCH_W), tm=2048).reshape(b, s, HEADS, HEAD_W)
    k_rope = jnp.broadcast_to(kr[:, :, None, :], (b, s, HEADS, MLA_ROPE))
    k_m = _rms(jnp.concatenate([k_nope, k_rope], axis=-1), lp['mla_kn_g'])
    hm = lambda x: jnp.transpose(x, (0, 2, 1, 3)).astype(BF16)
    idx = jnp.arange(t)
    causal = jnp.where(idx[None, :] <= idx[:, None], 0.0, NEG).astype(F32)
    zeros = jnp.zeros((t, t), F32)
    o_mla = _flash(hm(q_m * MLA_QK ** -0.5), hm(k_m), hm(v_m), jnp.stack([causal, zeros, zeros])[None], t=t)
    o_mla = jnp.transpose(o_mla, (0, 2, 1, 3)).reshape(b, s, BRANCH_W)
    maps = lambda x: jnp.transpose(x.reshape(b, s, HEADS, 2, DIFF_D), (0, 2, 3, 1, 4)).reshape(b, 2 * HEADS, s, DIFF_D)
    bias = jnp.stack([_rel_bias(rel_table, idx, idx) + causal[None],
                      _rel_bias(rel_table, t + idx, idx),
                      _rel_bias(rel_table, 2 * t + idx, idx)], axis=1)
    o = _flash(maps(q_d * DIFF_D ** -0.5).astype(BF16), maps(k_d).astype(BF16), hm(v_d), bias, t=t)
    o = o.reshape(b, HEADS, 2, s, HEAD_W)
    o_dif = jnp.transpose(o[:, :, 0] - lam * o[:, :, 1], (0, 2, 1, 3))
    return o_mla, o_dif


def _sample_attention(q_m, ckv, kr, q_d, k_d, v_d, lam, lp, rel_table, page_table, caches, layer):
    b, l = q_m.shape[:2]
    ckv_pool, kr_pool, dk_pool, dv_pool = caches
    n_past = page_table.shape[1] * PAGE_SIZE
    pos = n_past + jnp.arange(l)
    tok = jnp.arange(SUBLANES)
    new_ok = (tok[None, :] <= tok[:, None]) & (tok[None, :] < l)
    mask_new = jnp.where(new_ok, 0.0, NEG).astype(F32)
    qg = q_m * lp['mla_kn_g'] * MLA_QK ** -0.5
    w_uk = lp['mla_w_uk']
    eye = jnp.eye(HEADS, dtype=F32)
    w_bd = jnp.einsum('rhd,hg->hdgr', w_uk, eye).reshape(BRANCH_W, HEADS * MLA_KV_LORA)
    qa = _matmul(qg[..., :MLA_NOPE].reshape(b * l, BRANCH_W), w_bd, precision=HIGHEST)
    qa = jnp.transpose(qa.reshape(b, l, HEADS, MLA_KV_LORA), (0, 2, 1, 3))
    qr = jnp.transpose(qg[..., MLA_NOPE:], (0, 2, 1, 3))
    rows = HEADS * SUBLANES
    qa = _pad_rows(qa, 2, SUBLANES).reshape(b, rows, MLA_KV_LORA).astype(BF16)
    qr = _pad_rows(qr, 2, SUBLANES).reshape(b, rows, MLA_ROPE).astype(BF16)
    wt = jnp.transpose(w_uk.reshape(MLA_KV_LORA, BRANCH_W)).astype(BF16)
    wuv = jnp.transpose(lp['mla_w_uv'], (1, 0, 2)).astype(BF16)
    o = _mla_decode(page_table, qa, qr, wt, wuv, _pad_rows(ckv, 1, SUBLANES),
                    jnp.transpose(_pad_rows(kr, 1, SUBLANES), (0, 2, 1)),
                    jnp.tile(mask_new, (HEADS, 1)), ckv_pool, kr_pool, layer=layer)
    o_mla = jnp.transpose(o.reshape(b, HEADS, SUBLANES, HEAD_W)[:, :, :l], (0, 2, 1, 3)).reshape(b, l, BRANCH_W)
    q5 = jnp.transpose(q_d.reshape(b, l, HEADS, 2, DIFF_D), (0, 2, 3, 1, 4)) * DIFF_D ** -0.5
    q5 = _pad_rows(q5, 3, SUBLANES)
    eye2 = jnp.eye(2, dtype=F32)
    q_bd = jnp.einsum('bhmtd,hg,mn->bhmtgnd', q5, eye, eye2).reshape(b, 2 * rows, BRANCH_W).astype(BF16)
    pc = 16
    n_last = pc * PAGE_SIZE
    expand = lambda x: jnp.broadcast_to(_pad_rows(x, 1, SUBLANES)[:, None], (HEADS, 2, SUBLANES, x.shape[-1])
                                        ).reshape(2 * rows, x.shape[-1])
    bias_last = expand(_rel_bias(rel_table, pos, jnp.arange(n_past - n_last, n_past)))
    bias_far = expand(_rel_bias(rel_table, pos, jnp.arange(1)))
    bias_new = _rel_bias(rel_table, n_past + tok, n_past + tok) + mask_new[None]
    bias_new = jnp.broadcast_to(bias_new[:, None], (HEADS, 2, SUBLANES, SUBLANES)).reshape(2 * rows, SUBLANES)
    flat_t = lambda x: jnp.transpose(_pad_rows(x.reshape(b, l, BRANCH_W), 1, SUBLANES), (0, 2, 1))
    o = _dif_decode(page_table, q_bd, flat_t(k_d), flat_t(v_d), bias_far, bias_last, bias_new, dk_pool, dv_pool,
                    layer=layer, pc=pc)
    o = o.reshape(b, HEADS, 2, SUBLANES, HEADS, HEAD_W)
    o = jnp.stack([o[:, h, :, :, h] for h in range(HEADS)], axis=1)
    o_dif = jnp.transpose(o[:, :, 0, :l] - lam * o[:, :, 1, :l], (0, 2, 1, 3))
    return o_mla, o_dif


def _group_branches(z, pos, layer, lp, rel_table, st, paged):
    b, l, _ = z.shape
    zrw = z[..., Z_RW:Z_RW + RW_SHIFT_W]
    zmq, zmkv = z[..., Z_MQ:Z_MQ + MLA_Q_LORA], z[..., Z_MKV:Z_MKV + MLA_KV_LORA]
    zqkv, zz, zdif = z[..., Z_QKV:Z_QKV + GDN_QKV_W], z[..., Z_ZZ:Z_ZZ + BRANCH_W], z[..., Z_DIF:Z_DIF + DIFF_QKV_W]
    zmkr = z[..., Z_SMALL:Z_SMALL + MLA_ROPE]
    zb = z[..., Z_SMALL + MLA_ROPE:Z_SMALL + MLA_ROPE + HEADS]
    za = z[..., Z_SMALL + MLA_ROPE + HEADS:Z_SMALL + MLA_ROPE + 2 * HEADS]
    prompt = paged is None
    o_rw, rw_s, rw_shift = _rwkv_branch(zrw, st['rwkv_shift'], st['rwkv'], lp,
                                        bb=b if prompt else 8, tc=512 if prompt else l)
    o_gdn, gdn_s, gdn_conv = _gdn_branch(zqkv, zb, za, zz, st['gdn_conv'], st['gdn'], lp, bb=2)
    q_m, ckv, kr = _mla_project(zmq, zmkv, zmkr, pos, lp)
    q_d, k_d, v_d = _diff_project(zdif, lp)
    lam, lam_init = _diff_lambda(lp['dif_lam'], layer)
    if prompt:
        o_mla, o_dif = _prompt_attention(q_m, ckv, kr, q_d, k_d, v_d, lam, lp, rel_table, t=512)
    else:
        page_table, caches = paged
        o_mla, o_dif = _sample_attention(q_m, ckv, kr, q_d, k_d, v_d, lam, lp, rel_table, page_table, caches, layer)
    o_dif = (_rms(o_dif, lp['dif_subln_g']) * (1.0 - lam_init)).reshape(b, l, BRANCH_W)
    br = jnp.concatenate([o_rw, o_mla, o_gdn, o_dif], axis=-1).reshape(b * l, N_BRANCH * BRANCH_W)
    new = dict(mla_ckv=ckv, mla_krope=kr, diff_k=k_d, diff_v=v_d,
               rwkv=rw_s, rwkv_shift=rw_shift, gdn=gdn_s, gdn_conv=gdn_conv)
    return br, new


STATE_KEYS = ('mla_ckv', 'mla_krope', 'diff_k', 'diff_v', 'rwkv', 'rwkv_shift', 'gdn', 'gdn_conv')


def kernel(x_prompt, x_sample, cache_mla_ckv, cache_mla_krope, cache_diff_k, cache_diff_v, state_rwkv, state_rwkv_shift, state_gdn, state_gdn_conv, page_table, rel_bias, norm1_g, norm2_g, w_in, rw_mu, rw_w0, rw_w_up, rw_a0, rw_a_up, rw_g_up, rw_k_k, rw_k_a, rw_r_k, rw_ln_g, rw_ln_b, mla_q_norm_g, mla_w_uq, mla_kv_norm_g, mla_w_uk, mla_w_uv, mla_qn_g, mla_kn_g, gdn_conv_w, gdn_a_log, gdn_dt_bias, gdn_norm_g, dif_qn_g, dif_kn_g, dif_lam, dif_subln_g, w_branch, w_out, ffn_w_gate, ffn_w_up, ffn_w_down, moe_router, moe_w_gate, moe_w_up, moe_w_down):
    layer_params = dict(
        rw_mu=rw_mu, rw_w0=rw_w0, rw_w_up=rw_w_up, rw_a0=rw_a0, rw_a_up=rw_a_up, rw_g_up=rw_g_up,
        rw_k_k=rw_k_k, rw_k_a=rw_k_a, rw_r_k=rw_r_k, rw_ln_g=rw_ln_g, rw_ln_b=rw_ln_b,
        mla_q_norm_g=mla_q_norm_g, mla_w_uq=mla_w_uq, mla_kv_norm_g=mla_kv_norm_g, mla_w_uk=mla_w_uk,
        mla_w_uv=mla_w_uv, mla_qn_g=mla_qn_g, mla_kn_g=mla_kn_g, gdn_conv_w=gdn_conv_w, gdn_a_log=gdn_a_log,
        gdn_dt_bias=gdn_dt_bias, gdn_norm_g=gdn_norm_g, dif_qn_g=dif_qn_g, dif_kn_g=dif_kn_g, dif_lam=dif_lam,
        dif_subln_g=dif_subln_g)
    bp, sp, _ = x_prompt.shape
    bs, ls, _ = x_sample.shape
    tp, ts = bp * sp, bs * ls
    depth = w_in.shape[0]
    x = jnp.concatenate([x_prompt.reshape(tp, D_MODEL), x_sample.reshape(ts, D_MODEL)], axis=0)
    pos_p = jnp.arange(sp)
    pos_s = page_table.shape[1] * PAGE_SIZE + jnp.arange(ls)
    feat_major = lambda c: jnp.swapaxes(c.reshape(c.shape[:3] + (-1,)), 2, 3)
    caches = (cache_mla_ckv, feat_major(cache_mla_krope), feat_major(cache_diff_k), feat_major(cache_diff_v))
    sample_state = dict(rwkv=state_rwkv, rwkv_shift=state_rwkv_shift, gdn=state_gdn, gdn_conv=state_gdn_conv)
    got_p = {name: [] for name in STATE_KEYS}
    got_s = {name: [] for name in STATE_KEYS}
    for l in range(depth):
        lp = {name: arr[l] for name, arr in layer_params.items()}
        z = _rms_matmul(x, norm1_g[l], _pack_w_in(w_in[l]), tm=528, tn=2432, name="in_proj")
        st_p = dict(rwkv=jnp.zeros((bp, HEADS, HEAD_W, HEAD_W), F32), rwkv_shift=jnp.zeros((bp, RW_SHIFT_W), F32),
                    gdn=jnp.zeros((bp, HEADS, HEAD_W, HEAD_W), F32),
                    gdn_conv=jnp.zeros((bp, GDN_CONV - 1, GDN_QKV_W), F32))
        st_s = {name: arr[l] for name, arr in sample_state.items()}
        br_p, new_p = _group_branches(z[:tp].reshape(bp, sp, Z_W), pos_p, l, lp, rel_bias, st_p, None)
        br_s, new_s = _group_branches(z[tp:].reshape(bs, ls, Z_W), pos_s, l, lp, rel_bias, st_s, (page_table, caches))
        x = _merge_out(x, z, jnp.concatenate([br_p, br_s], axis=0), w_branch[l].astype(BF16), w_out[l].astype(BF16))
        if l % 2 == 0:
            combine = jnp.ones((1, tp + ts, 1), F32)
            x = _ffn(x, norm2_g[l], combine, ffn_w_gate[l // 2][None].astype(BF16), ffn_w_up[l // 2][None].astype(BF16),
                     ffn_w_down[l // 2][None].astype(BF16), tm=1056, tf=1408)
        else:
            router = jnp.pad(moe_router[l // 2], ((0, 0), (0, LANES - N_EXPERTS)))
            logits = _rms_matmul(x, norm2_g[l], router, tm=528, precision=HIGHEST, name="router")[:, :N_EXPERTS]
            top_v, top_i = lax.top_k(logits, TOP_K)
            top_w = jax.nn.softmax(top_v, axis=-1)
            combine = jnp.sum(jax.nn.one_hot(top_i, N_EXPERTS, dtype=F32) * top_w[..., None], axis=-2)
            combine = jnp.transpose(combine)[:, :, None]
            x = _ffn(x, norm2_g[l], combine, moe_w_gate[l // 2].astype(BF16), moe_w_up[l // 2].astype(BF16),
                     moe_w_down[l // 2].astype(BF16), tm=1056, tf=896)
        for name in STATE_KEYS:
            got_p[name].append(new_p[name])
            got_s[name].append(new_s[name])
    sp_out = {name: jnp.stack(v, axis=0) for name, v in got_p.items()}
    ss_out = {name: jnp.stack(v, axis=0) for name, v in got_s.items()}
    return ((x[:tp].reshape(bp, sp, D_MODEL), x[tp:].reshape(bs, ls, D_MODEL))
            + tuple(sp_out[name] for name in STATE_KEYS) + tuple(ss_out[name] for name in STATE_KEYS))
```

```python
import functools
import math

import jax
import jax.numpy as jnp
from jax import lax
from jax.experimental import pallas as pl
from jax.experimental.pallas import tpu as pltpu

F32 = jnp.float32
BF16 = jnp.bfloat16
HIGHEST = lax.Precision.HIGHEST
NEG = -1e30

D_MODEL = 1024
DEPTH = 2
PAST_LEN = 16384
PAGE_SIZE = 128
EPS = 1e-6
N_BRANCH = 4
BRANCH_W = 256
HEADS = 4
HEAD_W = 64
RW_DECAY_LORA, RW_AAA_LORA, RW_GATE_LORA = 32, 32, 64
RW_SHIFT_W = 3 * BRANCH_W + RW_DECAY_LORA + RW_AAA_LORA + RW_GATE_LORA
RW_LN_EPS = 64e-5
MLA_NOPE, MLA_ROPE = 64, 32
MLA_QK = MLA_NOPE + MLA_ROPE
MLA_Q_LORA, MLA_KV_LORA = 256, 128
ROPE_BASE = 10000.0
GDN_CONV = 4
GDN_CHUNK = 64
RW_CHUNK = 64
GDN_QKV_W = 768
DIFF_D = 32
DIFF_QKV_W = 768
REL_BUCKETS, REL_MAX_DIST = 32, 128
N_EXPERTS, TOP_K = 8, 2
PROJ_SIZES = (N_BRANCH * D_MODEL, RW_SHIFT_W, MLA_Q_LORA, MLA_KV_LORA, MLA_ROPE,
              GDN_QKV_W, HEADS, HEADS, BRANCH_W, DIFF_QKV_W)
Z_GATE, Z_RW, Z_MQ, Z_MKV, Z_QKV, Z_ZZ, Z_DIF, Z_SMALL = 0, 4096, 4992, 5248, 5376, 6144, 6400, 7168
Z_W = 7296

VMEM_LIMIT = 56 * 1024 * 1024
LANES = 128
SUBLANES = 8


def _params(sem, vmem=VMEM_LIMIT):
    return pltpu.CompilerParams(dimension_semantics=sem, vmem_limit_bytes=vmem)


def _row_tile(m, target):
    if m <= target:
        return m
    best = None
    for t in range(SUBLANES, target + 1, SUBLANES):
        if m % t == 0:
            best = t
    assert best is not None, (m, target)
    return best


def _mm_kernel(x_ref, w_ref, o_ref, *, precision):
    x, w = x_ref[...], w_ref[...]
    if precision is None:
        x, w = x.astype(BF16), w.astype(BF16)
    o_ref[...] = jnp.dot(x, w, preferred_element_type=F32, precision=precision)


def _matmul(x, w, *, tm=512, tn=None, precision=None, name="matmul"):
    m, k = x.shape
    n = w.shape[1]
    tm = _row_tile(m, tm)
    tn = tn or n
    return pl.pallas_call(
        functools.partial(_mm_kernel, precision=precision),
        out_shape=jax.ShapeDtypeStruct((m, n), F32),
        grid=(n // tn, m // tm),
        in_specs=[pl.BlockSpec((tm, k), lambda j, i: (i, 0)),
                  pl.BlockSpec((k, tn), lambda j, i: (0, j))],
        out_specs=pl.BlockSpec((tm, tn), lambda j, i: (i, j)),
        compiler_params=_params(("parallel", "parallel")),
        name=name,
    )(x, w)


def _rms_rows(x, g):
    return x * lax.rsqrt(jnp.mean(x * x, axis=-1, keepdims=True) + EPS) * g


def _rms_mm_kernel(x_ref, g_ref, w_ref, o_ref, *, precision):
    h = _rms_rows(x_ref[...], g_ref[...])
    w = w_ref[...]
    if precision is None:
        h, w = h.astype(BF16), w.astype(BF16)
    o_ref[...] = jnp.dot(h, w, preferred_element_type=F32, precision=precision)


def _rms_matmul(x, g, w, *, tm, tn=None, precision=None, name="rms_matmul"):
    m, k = x.shape
    n = w.shape[1]
    tm = _row_tile(m, tm)
    tn = tn or n
    return pl.pallas_call(
        functools.partial(_rms_mm_kernel, precision=precision),
        out_shape=jax.ShapeDtypeStruct((m, n), F32),
        grid=(n // tn, m // tm),
        in_specs=[pl.BlockSpec((tm, k), lambda j, i: (i, 0)),
                  pl.BlockSpec((1, k), lambda j, i: (0, 0)),
                  pl.BlockSpec((k, tn), lambda j, i: (0, j))],
        out_specs=pl.BlockSpec((tm, tn), lambda j, i: (i, j)),
        compiler_params=_params(("parallel", "parallel")),
        name=name,
    )(x, g.reshape(1, k), w)


def _merge_kernel(x_ref, zg_ref, br_ref, wb_ref, wo_ref, o_ref):
    br = br_ref[...].astype(BF16)
    merged = None
    for n in range(N_BRANCH):
        gate = jax.nn.sigmoid(zg_ref[:, n * D_MODEL:(n + 1) * D_MODEL])
        p = jnp.dot(br[:, n * BRANCH_W:(n + 1) * BRANCH_W], wb_ref[n], preferred_element_type=F32)
        merged = gate * p if merged is None else merged + gate * p
    o_ref[...] = x_ref[...] + jnp.dot(merged.astype(BF16), wo_ref[...], preferred_element_type=F32)


def _merge_out(x, z, br, wb, wo, *, tm=528):
    m = x.shape[0]
    tm = _row_tile(m, tm)
    return pl.pallas_call(
        _merge_kernel,
        out_shape=jax.ShapeDtypeStruct((m, D_MODEL), F32),
        grid=(m // tm,),
        in_specs=[pl.BlockSpec((tm, D_MODEL), lambda i: (i, 0)),
                  pl.BlockSpec((tm, N_BRANCH * D_MODEL), lambda i: (i, 0)),
                  pl.BlockSpec((tm, N_BRANCH * BRANCH_W), lambda i: (i, 0)),
                  pl.BlockSpec((N_BRANCH, BRANCH_W, D_MODEL), lambda i: (0, 0, 0)),
                  pl.BlockSpec((D_MODEL, D_MODEL), lambda i: (0, 0))],
        out_specs=pl.BlockSpec((tm, D_MODEL), lambda i: (i, 0)),
        compiler_params=_params(("parallel",)),
        name="merge_out",
    )(x, z, br, wb, wo)


def _ffn_kernel(x_ref, g_ref, c_ref, wg_ref, wu_ref, wd_ref, o_ref, h_sc, acc_sc):
    e, j = pl.program_id(1), pl.program_id(2)

    @pl.when((e == 0) & (j == 0))
    def _():
        h_sc[...] = _rms_rows(x_ref[...], g_ref[...]).astype(BF16)
        acc_sc[...] = jnp.zeros_like(acc_sc)

    h = h_sc[...]
    a = jnp.dot(h, wg_ref[0], preferred_element_type=F32)
    u = jnp.dot(h, wu_ref[0], preferred_element_type=F32)
    act = (jax.nn.silu(a) * u).astype(BF16)
    acc_sc[...] += c_ref[0] * jnp.dot(act, wd_ref[0], preferred_element_type=F32)

    @pl.when((e == pl.num_programs(1) - 1) & (j == pl.num_programs(2) - 1))
    def _():
        o_ref[...] = x_ref[...] + acc_sc[...]


def _ffn(x, g, combine, wg, wu, wd, *, tm, tf):
    m = x.shape[0]
    n_e, _, f = wg.shape
    tm = _row_tile(m, tm)
    return pl.pallas_call(
        _ffn_kernel,
        out_shape=jax.ShapeDtypeStruct((m, D_MODEL), F32),
        grid=(m // tm, n_e, f // tf),
        in_specs=[pl.BlockSpec((tm, D_MODEL), lambda i, e, j: (i, 0)),
                  pl.BlockSpec((1, D_MODEL), lambda i, e, j: (0, 0)),
                  pl.BlockSpec((1, tm, 1), lambda i, e, j: (e, i, 0)),
                  pl.BlockSpec((1, D_MODEL, tf), lambda i, e, j: (e, 0, j)),
                  pl.BlockSpec((1, D_MODEL, tf), lambda i, e, j: (e, 0, j)),
                  pl.BlockSpec((1, tf, D_MODEL), lambda i, e, j: (e, j, 0))],
        out_specs=pl.BlockSpec((tm, D_MODEL), lambda i, e, j: (i, 0)),
        scratch_shapes=[pltpu.VMEM((tm, D_MODEL), BF16), pltpu.VMEM((tm, D_MODEL), F32)],
        compiler_params=_params(("parallel", "arbitrary", "arbitrary")),
        name="ffn",
    )(x, g.reshape(1, D_MODEL), combine, wg, wu, wd)


FLASH_TILE = 1024
FLASH_ROWS = 256


def _flash_kernel(q_ref, k_ref, v_ref, b_ref, o_ref, m_sc, l_sc, acc_sc, *, t):
    qi = pl.program_id(2)
    m_sc[...] = jnp.full_like(m_sc, NEG)
    l_sc[...] = jnp.zeros_like(l_sc)
    acc_sc[...] = jnp.zeros_like(acc_sc)

    def body(kj, carry):
        off = pl.multiple_of(kj * t, t)
        k = k_ref[0, 0, pl.ds(off, t), :]
        v = v_ref[0, 0, pl.ds(off, t), :]
        tile = jnp.minimum(qi - kj, 2)
        for rb in range(t // FLASH_ROWS):
            rows = slice(rb * FLASH_ROWS, (rb + 1) * FLASH_ROWS)
            s = lax.dot_general(q_ref[0, 0, rows, :], k, (((1,), (1,)), ((), ())), preferred_element_type=F32)
            s = s + b_ref[0, tile, rows, :]
            m_prev = m_sc[rows, :]
            m_new = jnp.maximum(m_prev, jnp.max(s, axis=-1, keepdims=True))
            alpha = jnp.exp(m_prev - m_new)
            p = jnp.exp(s - m_new)
            l_sc[rows, :] = alpha * l_sc[rows, :] + jnp.sum(p, axis=-1, keepdims=True)
            acc_sc[rows, :] = alpha * acc_sc[rows, :] + jnp.dot(p.astype(BF16), v, preferred_element_type=F32)
            m_sc[rows, :] = m_new
        return carry

    lax.fori_loop(0, qi + 1, body, 0)
    o_ref[0, 0] = acc_sc[...] / l_sc[...]


def _flash(q, k, v, bias, *, t):
    b, n_map, s, d = q.shape
    hv, dv = v.shape[1], v.shape[3]
    rep = n_map // hv
    bias_idx = (lambda bi, h, i: (h // rep, 0, 0, 0)) if bias.shape[0] > 1 else (lambda bi, h, i: (0, 0, 0, 0))
    return pl.pallas_call(
        functools.partial(_flash_kernel, t=t),
        out_shape=jax.ShapeDtypeStruct((b, n_map, s, dv), F32),
        grid=(b, n_map, s // t),
        in_specs=[pl.BlockSpec((1, 1, t, d), lambda bi, h, i: (bi, h, i, 0)),
                  pl.BlockSpec((1, 1, s, d), lambda bi, h, i: (bi, h, 0, 0)),
                  pl.BlockSpec((1, 1, s, dv), lambda bi, h, i: (bi, h // rep, 0, 0)),
                  pl.BlockSpec((1, 3, t, t), bias_idx)],
        out_specs=pl.BlockSpec((1, 1, t, dv), lambda bi, h, i: (bi, h, i, 0)),
        scratch_shapes=[pltpu.VMEM((t, 1), F32), pltpu.VMEM((t, 1), F32), pltpu.VMEM((t, dv), F32)],
        compiler_params=_params(("parallel", "parallel", "arbitrary")),
        name="flash",
    )(q, k, v, bias)


_MM = (((1,), (0,)), ((), ()))
_MM_BT = (((1,), (1,)), ((), ()))
_MM_AT = (((0,), (0,)), ((), ()))
SOLVE_BASE = 8


def _split_bf16(a):
    hi = a.astype(BF16)
    return hi, (a - hi.astype(F32)).astype(BF16)


def _dot3(a, b, dims=_MM):
    ah, al = _split_bf16(a)
    bh, bl = _split_bf16(b)
    dg = lambda x, y: lax.dot_general(x, y, dims, preferred_element_type=F32)
    return dg(ah, bh) + (dg(ah, bl) + dg(al, bh))


def _dot_mask(mask, x, mask_first=True):
    m = mask.astype(BF16)
    hi = x.astype(BF16)
    rest = x - hi.astype(F32)
    mid = rest.astype(BF16)
    lo = (rest - mid.astype(F32)).astype(BF16)
    if mask_first:
        dg = lambda y: lax.dot_general(m, y, _MM, preferred_element_type=F32)
    else:
        dg = lambda y: lax.dot_general(y, m, _MM, preferred_element_type=F32)
    return dg(hi) + (dg(mid) + dg(lo))


def _block_ids(c):
    ii = lax.broadcasted_iota(jnp.int32, (c, c), 0)
    jj = lax.broadcasted_iota(jnp.int32, (c, c), 1)
    return ii, jj


def _same_block(ii, jj, size):
    shift = jnp.full(ii.shape, size.bit_length() - 1, ii.dtype)
    return lax.shift_right_logical(ii, shift) == lax.shift_right_logical(jj, shift)


def _each(f, *lists):
    return [f(*xs) for xs in zip(*lists)]


def _unit_lower_inverse(lows, c, ii, jj):
    eye = (ii == jj).astype(F32)
    base = _same_block(ii, jj, SOLVE_BASE)
    d = _each(lambda low: jnp.where(base, low, 0.0), lows)
    d2 = _each(lambda x: _dot3(x, x), d)
    d4 = _each(lambda x: _dot3(x, x), d2)
    inv = _each(lambda x, x2: _dot3(eye - x, eye + x2), d, d2)
    inv = _each(lambda x, x4: _dot3(x, eye + x4), inv, d4)
    size = SOLVE_BASE
    while size < c:
        pair, single = _same_block(ii, jj, 2 * size), _same_block(ii, jj, size)
        e = _each(lambda low: jnp.where(pair, jnp.where(single, 0.0, low), 0.0), lows)
        ie = _each(_dot3, inv, e)
        inv = _each(lambda x, y: x - _dot3(y, x), inv, ie)
        size *= 2
    return inv


def _gdn_kernel(q_ref, k_ref, v_ref, gcol_ref, grow_ref, bcol_ref, s0_ref, o_ref, sout_ref, s_sc, *, bb, c):
    @pl.when(pl.program_id(1) == 0)
    def _():
        s_sc[...] = s0_ref[...]

    ii, jj = _block_ids(c)
    causal, upper = ii >= jj, ii <= jj
    units = [(b, h) for b in range(bb) for h in range(HEADS)]
    get = lambda ref: [ref[b, h] for b, h in units]
    q = _each(lambda x: x * (HEAD_W ** -0.5), get(q_ref))
    k, v = get(k_ref), get(v_ref)
    beta = [bcol_ref[b, h, 0] for b, h in units]
    gc_c = [_dot_mask(causal, jnp.broadcast_to(gcol_ref[b, h, 0], (c, c))) for b, h in units]
    gc_r = [_dot_mask(upper, jnp.broadcast_to(grow_ref[b, h, 0], (c, c)), mask_first=False) for b, h in units]
    decay = _each(lambda x, y: jnp.where(causal, jnp.exp(jnp.where(causal, x - y, 0.0)), 0.0), gc_c, gc_r)
    gc = _each(lambda x: x[:, 0:1], gc_c)
    gc_last = _each(lambda x: x[c - 1:c, 0:1], gc_c)
    kb = _each(jnp.multiply, k, beta)
    a_low = _each(lambda x, y, d: jnp.where(ii > jj, _dot3(x, y, _MM_BT) * d, 0.0), kb, k, decay)
    rhs = _each(lambda vv, bb_, kk_, g: jnp.concatenate([vv * bb_, kk_ * jnp.exp(g)], axis=1), v, beta, kb, gc)
    x = _each(_dot3, _unit_lower_inverse(a_low, c, ii, jj), rhs)
    qk = _each(lambda x1, x2, d: _dot3(x1, x2, _MM_BT) * d, q, k, decay)
    s = [s_sc[b, h] for b, h in units]
    v_new = _each(lambda xx, st: xx[:, :HEAD_W] - _dot3(xx[:, HEAD_W:], st), x, s)
    o = _each(lambda qq, g, st, a, vn: _dot3(qq * jnp.exp(g), st) + _dot3(a, vn), q, gc, s, qk, v_new)
    s_new = _each(lambda st, gl, kk_, g, vn: st * jnp.exp(gl) + _dot3(kk_ * jnp.exp(gl - g), vn, _MM_AT),
                  s, gc_last, k, gc, v_new)
    for (b, h), oo, ss in zip(units, o, s_new):
        o_ref[b, h] = oo
        s_sc[b, h] = ss
    sout_ref[...] = s_sc[...]


def _gdn_chunks(q, k, v, g, beta, s0, *, c, bb):
    b, h, l, _ = q.shape
    n = l // c
    gcol = g.reshape(b, h, n, c, 1)
    grow = g.reshape(b, h, n, 1, c)
    bcol = beta.reshape(b, h, n, c, 1)
    seq = pl.BlockSpec((bb, h, c, HEAD_W), lambda i, j: (i, 0, j, 0))
    col = pl.BlockSpec((bb, h, 1, c, 1), lambda i, j: (i, 0, j, 0, 0))
    row = pl.BlockSpec((bb, h, 1, 1, c), lambda i, j: (i, 0, j, 0, 0))
    st = pl.BlockSpec((bb, h, HEAD_W, HEAD_W), lambda i, j: (i, 0, 0, 0))
    return pl.pallas_call(
        functools.partial(_gdn_kernel, bb=bb, c=c),
        out_shape=(jax.ShapeDtypeStruct((b, h, l, HEAD_W), F32),
                   jax.ShapeDtypeStruct((b, h, HEAD_W, HEAD_W), F32)),
        grid=(b // bb, n),
        in_specs=[seq, seq, seq, col, row, col, st],
        out_specs=(seq, st),
        scratch_shapes=[pltpu.VMEM((bb, h, HEAD_W, HEAD_W), F32)],
        compiler_params=_params(("parallel", "arbitrary")),
        name="gdn_chunks",
    )(q, k, v, gcol, grow, bcol, s0)


def _rwkv_chunk_kernel(r_ref, lw_ref, k_ref, v_ref, kk_ref, kka_ref, s0_ref, y_ref, sout_ref, s_sc, *, bb, c):
    @pl.when(pl.program_id(1) == 0)
    def _():
        s_sc[...] = s0_ref[...]

    ii, jj = _block_ids(c)
    incl, strict = ii >= jj, ii > jj
    units = [(b, h) for b in range(bb) for h in range(HEADS)]
    get = lambda ref: [ref[b, h] for b, h in units]
    r, lw, k, v, kap, bq = get(r_ref), get(lw_ref), get(k_ref), get(v_ref), get(kk_ref), get(kka_ref)
    lam = _each(lambda x: _dot_mask(incl, x), lw)
    lam_end = _each(lambda x: x[c - 1:c, :], lam)
    e_neg = _each(lambda x: jnp.exp(-x), lam)
    kap_t = _each(lambda x, la, lg: x * jnp.exp(la - lg), kap, lam, lw)
    b_t = _each(jnp.multiply, bq, e_neg)
    k_t = _each(jnp.multiply, k, e_neg)
    r_t = _each(lambda x, la: x * jnp.exp(la), r, lam)
    e_end = _each(lambda le, la: jnp.exp(le - la), lam_end, lam)
    bt_dot = lambda x, y: _dot3(x, y, _MM_BT)
    l_b = _each(lambda x, y: jnp.where(strict, bt_dot(x, y), 0.0), kap_t, b_t)
    l_k = _each(lambda x, y: jnp.where(strict, bt_dot(x, y), 0.0), kap_t, k_t)
    a_rk = _each(lambda x, y: jnp.where(incl, bt_dot(x, y), 0.0), r_t, k_t)
    a_rb = _each(lambda x, y: jnp.where(incl, bt_dot(x, y), 0.0), r_t, b_t)
    t_inv = _unit_lower_inverse(l_b, c, ii, jj)
    p1 = _each(_dot3, t_inv, kap_t)
    p2 = _each(_dot3, t_inv, _each(_dot3, l_k, v))
    y_local = _each(_dot3, a_rk, v)
    s_local = _each(lambda x, y, e: _dot3(x, y * e, _MM_AT), v, k, e_end)
    s = [s_sc[b, h] for b, h in units]
    u = _each(lambda x, st, y: bt_dot(x, st) + y, p1, s, p2)
    y = _each(lambda x, st, yl, a, uu: bt_dot(x, st) + yl - _dot3(a, uu), r_t, s, y_local, a_rb, u)
    s_new = _each(lambda st, le, sl, uu, x, e: st * jnp.exp(le) + sl - _dot3(uu, x * e, _MM_AT),
                  s, lam_end, s_local, u, bq, e_end)
    for (b, h), yy, ss in zip(units, y, s_new):
        y_ref[b, h] = yy
        s_sc[b, h] = ss
    sout_ref[...] = s_sc[...]


def _rwkv_chunks(r, lw, k, v, kk, kka, s0, *, c, bb):
    b, h, l, _ = r.shape
    seq = pl.BlockSpec((bb, h, c, HEAD_W), lambda i, j: (i, 0, j, 0))
    st = pl.BlockSpec((bb, h, HEAD_W, HEAD_W), lambda i, j: (i, 0, 0, 0))
    return pl.pallas_call(
        functools.partial(_rwkv_chunk_kernel, bb=bb, c=c),
        out_shape=(jax.ShapeDtypeStruct((b, h, l, HEAD_W), F32),
                   jax.ShapeDtypeStruct((b, h, HEAD_W, HEAD_W), F32)),
        grid=(b // bb, l // c),
        in_specs=[seq] * 6 + [st],
        out_specs=(seq, st),
        scratch_shapes=[pltpu.VMEM((bb, h, HEAD_W, HEAD_W), F32)],
        compiler_params=_params(("parallel", "arbitrary")),
        name="rwkv_chunks",
    )(r, lw, k, v, kk, kka, s0)


def _page_copy(pool, buf, sem, layer, page, slot, p, tokens_on_lanes):
    if tokens_on_lanes:
        dst = buf.at[slot, :, pl.ds(p * PAGE_SIZE, PAGE_SIZE)]
    else:
        dst = buf.at[slot, pl.ds(p * PAGE_SIZE, PAGE_SIZE), :]
    return pltpu.make_async_copy(pool.at[layer, page], dst, sem.at[slot])


def _paged_stream(pt_ref, pools, bufs, sems, lanes, *, layer, pc, n_chunks, process):
    b = pl.program_id(0)
    assert n_chunks % 2 == 0

    def fetch(seq, chunk, slot):
        for p in range(pc):
            page = pt_ref[seq, chunk * pc + p]
            for pool, buf, sem, on_lanes in zip(pools, bufs, sems, lanes):
                _page_copy(pool, buf, sem, layer, page, slot, p, on_lanes).start()

    def wait(slot):
        for p in range(pc):
            for pool, buf, sem, on_lanes in zip(pools, bufs, sems, lanes):
                _page_copy(pool, buf, sem, layer, 0, slot, p, on_lanes).wait()

    @pl.when(b == 0)
    def _():
        fetch(0, 0, 0)

    for ch in range(n_chunks):
        slot = ch % 2
        if ch + 1 < n_chunks:
            fetch(b, ch + 1, 1 - slot)
        else:
            @pl.when(b + 1 < pl.num_programs(0))
            def _():
                fetch(b + 1, 0, 1 - slot)
        wait(slot)
        process(ch, slot)


def _online_softmax_step(s, values, m_sc, l_sc, acc_sc, values_transposed):
    m_prev = m_sc[...]
    m_new = jnp.maximum(m_prev, jnp.max(s, axis=-1, keepdims=True))
    alpha = jnp.exp(m_prev - m_new)
    p = jnp.exp(s - m_new)
    l_sc[...] = alpha * l_sc[...] + jnp.sum(p, axis=-1, keepdims=True)
    pv = lax.dot_general(p.astype(BF16), values, _MM_BT if values_transposed else _MM, preferred_element_type=F32)
    acc_sc[...] = alpha * acc_sc[...] + pv
    m_sc[...] = m_new


def _mla_dec_kernel(pt_ref, qa_ref, qr_ref, wt_ref, wuv_ref, cn_ref, rn_ref, mn_ref, ckv_hbm, kr_hbm, o_ref,
                    cbuf, rbuf, sem_c, sem_r, m_sc, l_sc, acc_sc, *, layer, pc, n_chunks):
    def process(c, r_t, mask):
        cb = c.astype(BF16)
        knt = lax.dot_general(wt_ref[...], cb, _MM_BT, preferred_element_type=F32)
        kn2 = knt * knt
        ssr = jnp.sum(r_t * r_t, axis=0, keepdims=True)
        s_raw = (lax.dot_general(qa_ref[0], cb, _MM_BT, preferred_element_type=F32)
                 + jnp.dot(qr_ref[0], r_t.astype(BF16), preferred_element_type=F32))
        parts = []
        for h in range(HEADS):
            ss = jnp.sum(kn2[h * HEAD_W:(h + 1) * HEAD_W], axis=0, keepdims=True) + ssr
            rinv = lax.rsqrt(ss * (1.0 / MLA_QK) + EPS)
            parts.append(s_raw[h * SUBLANES:(h + 1) * SUBLANES] * rinv)
        s = jnp.concatenate(parts, axis=0)
        if mask is not None:
            s = s + mask
        _online_softmax_step(s, cb, m_sc, l_sc, acc_sc, values_transposed=False)

    m_sc[...] = jnp.full_like(m_sc, NEG)
    l_sc[...] = jnp.zeros_like(l_sc)
    acc_sc[...] = jnp.zeros_like(acc_sc)
    _paged_stream(pt_ref, [ckv_hbm, kr_hbm], [cbuf, rbuf], [sem_c, sem_r], [False, True],
                  layer=layer, pc=pc, n_chunks=n_chunks,
                  process=lambda ch, slot: process(cbuf[slot], rbuf[slot], None))
    process(cn_ref[0], rn_ref[0], mn_ref[...])
    ctx = (acc_sc[...] / l_sc[...]).astype(BF16)
    for h in range(HEADS):
        o_ref[0, h * SUBLANES:(h + 1) * SUBLANES, :] = jnp.dot(
            ctx[h * SUBLANES:(h + 1) * SUBLANES], wuv_ref[h], preferred_element_type=F32)


def _mla_decode(page_table, qa, qr, wt, wuv, c_new, r_new, mask_new, ckv_pool, kr_pool, *, layer, pc=16):
    b, n_pages = page_table.shape
    rows = HEADS * SUBLANES
    n_chunks = n_pages // pc
    n = pc * PAGE_SIZE
    per_seq = lambda shape: pl.BlockSpec((1,) + shape, lambda i, pt: (i, 0, 0))
    whole = lambda shape: pl.BlockSpec(shape, lambda i, pt: (0,) * len(shape))
    return pl.pallas_call(
        functools.partial(_mla_dec_kernel, layer=layer, pc=pc, n_chunks=n_chunks),
        out_shape=jax.ShapeDtypeStruct((b, rows, HEAD_W), F32),
        grid_spec=pltpu.PrefetchScalarGridSpec(
            num_scalar_prefetch=1,
            grid=(b,),
            in_specs=[per_seq((rows, MLA_KV_LORA)), per_seq((rows, MLA_ROPE)),
                      whole((BRANCH_W, MLA_KV_LORA)), whole((HEADS, MLA_KV_LORA, HEAD_W)),
                      per_seq((SUBLANES, MLA_KV_LORA)), per_seq((MLA_ROPE, SUBLANES)),
                      whole((rows, SUBLANES)),
                      pl.BlockSpec(memory_space=pl.ANY), pl.BlockSpec(memory_space=pl.ANY)],
            out_specs=per_seq((rows, HEAD_W)),
            scratch_shapes=[pltpu.VMEM((2, n, MLA_KV_LORA), F32), pltpu.VMEM((2, MLA_ROPE, n), F32),
                            pltpu.SemaphoreType.DMA((2,)), pltpu.SemaphoreType.DMA((2,)),
                            pltpu.VMEM((rows, 1), F32), pltpu.VMEM((rows, 1), F32),
                            pltpu.VMEM((rows, MLA_KV_LORA), F32)]),
        compiler_params=_params(("arbitrary",)),
        name="mla_decode",
    )(page_table, qa, qr, wt, wuv, c_new, r_new, mask_new, ckv_pool, kr_pool)


def _dif_dec_kernel(pt_ref, q_ref, kn_ref, vn_ref, bfar_ref, blast_ref, bnew_ref, dk_hbm, dv_hbm, o_ref,
                    kbuf, vbuf, sem_k, sem_v, m_sc, l_sc, acc_sc, *, layer, pc, n_chunks):
    def process(k_t, v_t, bias):
        s = jnp.dot(q_ref[0], k_t.astype(BF16), preferred_element_type=F32) + bias
        _online_softmax_step(s, v_t.astype(BF16), m_sc, l_sc, acc_sc, values_transposed=True)

    m_sc[...] = jnp.full_like(m_sc, NEG)
    l_sc[...] = jnp.zeros_like(l_sc)
    acc_sc[...] = jnp.zeros_like(acc_sc)
    _paged_stream(pt_ref, [dk_hbm, dv_hbm], [kbuf, vbuf], [sem_k, sem_v], [True, True],
                  layer=layer, pc=pc, n_chunks=n_chunks,
                  process=lambda ch, slot: process(kbuf[slot], vbuf[slot],
                                                   blast_ref[...] if ch == n_chunks - 1 else bfar_ref[...]))
    process(kn_ref[0], vn_ref[0], bnew_ref[...])
    o_ref[0] = acc_sc[...] / l_sc[...]


def _dif_decode(page_table, q_bd, k_new, v_new, bias_far, bias_last, bias_new, dk_pool, dv_pool, *, layer, pc=16):
    b, n_pages = page_table.shape
    rows = q_bd.shape[1]
    n_chunks = n_pages // pc
    n = pc * PAGE_SIZE
    per_seq = lambda shape: pl.BlockSpec((1,) + shape, lambda i, pt: (i, 0, 0))
    whole = lambda shape: pl.BlockSpec(shape, lambda i, pt: (0,) * len(shape))
    return pl.pallas_call(
        functools.partial(_dif_dec_kernel, layer=layer, pc=pc, n_chunks=n_chunks),
        out_shape=jax.ShapeDtypeStruct((b, rows, BRANCH_W), F32),
        grid_spec=pltpu.PrefetchScalarGridSpec(
            num_scalar_prefetch=1,
            grid=(b,),
            in_specs=[per_seq((rows, BRANCH_W)), per_seq((BRANCH_W, SUBLANES)), per_seq((BRANCH_W, SUBLANES)),
                      whole((rows, 1)), whole((rows, n)), whole((rows, SUBLANES)),
                      pl.BlockSpec(memory_space=pl.ANY), pl.BlockSpec(memory_space=pl.ANY)],
            out_specs=per_seq((rows, BRANCH_W)),
            scratch_shapes=[pltpu.VMEM((2, BRANCH_W, n), F32), pltpu.VMEM((2, BRANCH_W, n), F32),
                            pltpu.SemaphoreType.DMA((2,)), pltpu.SemaphoreType.DMA((2,)),
                            pltpu.VMEM((rows, 1), F32), pltpu.VMEM((rows, 1), F32),
                            pltpu.VMEM((rows, BRANCH_W), F32)]),
        compiler_params=_params(("arbitrary",)),
        name="dif_decode",
    )(page_table, q_bd, k_new, v_new, bias_far, bias_last, bias_new, dk_pool, dv_pool)


def _rms(x, g):
    return x * lax.rsqrt(jnp.mean(x * x, axis=-1, keepdims=True) + EPS) * g


def _l2(x):
    return x * lax.rsqrt(jnp.maximum(jnp.sum(x * x, axis=-1, keepdims=True), 1e-12))


def _rope(x, pos):
    half = x.shape[-1] // 2
    inv = ROPE_BASE ** (-jnp.arange(half, dtype=F32) / half)
    ang = pos.astype(F32)[:, None] * inv[None, :]
    cos, sin = jnp.cos(ang)[:, None, :], jnp.sin(ang)[:, None, :]
    x1, x2 = x[..., :half], x[..., half:]
    return jnp.concatenate([x1 * cos - x2 * sin, x1 * sin + x2 * cos], axis=-1)


def _rel_bias(rel_table, qpos, kpos):
    n = jnp.maximum(qpos[:, None] - kpos[None, :], 0)
    exact = REL_BUCKETS // 2
    nf = jnp.maximum(n, 1).astype(F32)
    large = exact + (jnp.log(nf / exact) / math.log(REL_MAX_DIST / exact) * (REL_BUCKETS - exact)).astype(jnp.int32)
    bucket = jnp.where(n < exact, n, jnp.minimum(large, REL_BUCKETS - 1))
    table = rel_table.astype(F32)
    return sum(jnp.where(bucket[None] == bkt, table[bkt][:, None, None], 0.0) for bkt in range(REL_BUCKETS))


def _pad_rows(x, axis, to):
    pad = [(0, 0)] * x.ndim
    pad[axis] = (0, to - x.shape[axis])
    return jnp.pad(x, pad)


def _pack_w_in(w_in):
    o, parts = 0, []
    for wdt in PROJ_SIZES:
        parts.append(w_in[:, o:o + wdt])
        o += wdt
    zg, zrw, zmq, zmkv, zmkr, zqkv, zb, za, zz, zdif = parts
    small = jnp.concatenate([zmkr, zb, za], axis=1)
    small = jnp.pad(small, ((0, 0), (0, Z_W - Z_SMALL - small.shape[1])))
    return jnp.concatenate([zg, zrw, zmq, zmkv, zqkv, zz, zdif, small], axis=1).astype(BF16)


def _rwkv_branch(zr, shift_prev, s0, lp, *, bb):
    b, l, _ = zr.shape
    prev = jnp.concatenate([shift_prev[:, None], zr[:, :-1]], axis=1)
    zs = (zr + (prev - zr) * lp['rw_mu']).reshape(b * l, RW_SHIFT_W)
    o1, o2, o3 = BRANCH_W, 2 * BRANCH_W, 3 * BRANCH_W
    r, k, v = zs[:, :o1], zs[:, o1:o2], zs[:, o2:o3]
    wd = zs[:, o3:o3 + RW_DECAY_LORA]
    ad = zs[:, o3 + RW_DECAY_LORA:o3 + RW_DECAY_LORA + RW_AAA_LORA]
    gd = zs[:, o3 + RW_DECAY_LORA + RW_AAA_LORA:]
    tm = 2048
    w_log = -jax.nn.softplus(-(lp['rw_w0'] + _matmul(jnp.tanh(wd), lp['rw_w_up'], tm=tm, precision=HIGHEST))) - 0.5
    log_decay = -jnp.exp(w_log)
    a = jax.nn.sigmoid(lp['rw_a0'] + _matmul(ad, lp['rw_a_up'], tm=tm, precision=HIGHEST))
    g = _matmul(jax.nn.sigmoid(gd), lp['rw_g_up'], tm=tm, precision=HIGHEST)
    heads = lambda t: t.reshape(b * l, HEADS, HEAD_W)
    kk = _l2(heads(k * lp['rw_k_k'])).reshape(b * l, BRANCH_W)
    k = k * (1.0 + (a - 1.0) * lp['rw_k_a'])
    c = math.gcd(l, RW_CHUNK)
    lpad = l
    if c < SUBLANES:
        lpad = c = SUBLANES
    hm = lambda t: _pad_rows(jnp.transpose(t.reshape(b, l, HEADS, HEAD_W), (0, 2, 1, 3)), 2, lpad)
    y, s_new = _rwkv_chunks(hm(r), hm(log_decay), hm(k), hm(v), hm(kk), hm(kk * a), s0, c=c, bb=bb)
    y = jnp.transpose(y[:, :, :l], (0, 2, 1, 3)).reshape(b * l, HEADS, HEAD_W)
    mu = jnp.mean(y, axis=-1, keepdims=True)
    var = jnp.mean(jnp.square(y - mu), axis=-1, keepdims=True)
    y = ((y - mu) * lax.rsqrt(var + RW_LN_EPS) * lp['rw_ln_g'].reshape(HEADS, HEAD_W)
         + lp['rw_ln_b'].reshape(HEADS, HEAD_W))
    y = y + jnp.sum(heads(r) * heads(k) * lp['rw_r_k'], axis=-1, keepdims=True) * heads(v)
    return (y.reshape(b * l, BRANCH_W) * g).reshape(b, l, BRANCH_W), s_new, zr[:, -1]


def _gdn_branch(zqkv, zb, za, zz, conv_prev, s0, lp, *, bb):
    b, l, _ = zqkv.shape
    xc = jnp.concatenate([conv_prev, zqkv], axis=1)
    wc = lp['gdn_conv_w']
    conv = sum(xc[:, j:j + l] * wc[j] for j in range(GDN_CONV))
    qkv = jax.nn.silu(conv)
    hm = lambda t: jnp.transpose(t.reshape(b, l, HEADS, HEAD_W), (0, 2, 1, 3))
    q = _l2(hm(qkv[..., :BRANCH_W]))
    k = _l2(hm(qkv[..., BRANCH_W:2 * BRANCH_W]))
    v = hm(qkv[..., 2 * BRANCH_W:])
    beta = jnp.transpose(jax.nn.sigmoid(zb), (0, 2, 1))
    g = jnp.transpose(-jnp.exp(lp['gdn_a_log']) * jax.nn.softplus(za + lp['gdn_dt_bias']), (0, 2, 1))
    c = math.gcd(l, GDN_CHUNK)
    lpad = l
    if c < SUBLANES:
        lpad = c = SUBLANES
        q, k, v = (_pad_rows(t, 2, lpad) for t in (q, k, v))
        g, beta = _pad_rows(g, 2, lpad), _pad_rows(beta, 2, lpad)
    o, s_new = _gdn_chunks(q, k, v, g, beta, s0, c=c, bb=bb)
    o = jnp.transpose(o[:, :, :l], (0, 2, 1, 3))
    o = _rms(o, lp['gdn_norm_g']) * jax.nn.silu(zz.reshape(b, l, HEADS, HEAD_W))
    return o.reshape(b, l, BRANCH_W), s_new, xc[:, -(GDN_CONV - 1):]


def _mla_project(zmq, zmkv, zmkr, pos, lp):
    b, l, _ = zmq.shape
    cq = _rms(zmq, lp['mla_q_norm_g']).reshape(b * l, MLA_Q_LORA)
    q = _matmul(cq, lp['mla_w_uq'].reshape(MLA_Q_LORA, HEADS * MLA_QK), tm=2048).reshape(b, l, HEADS, MLA_QK)
    q = jnp.concatenate([q[..., :MLA_NOPE], _rope(q[..., MLA_NOPE:], pos)], axis=-1)
    q = _rms(q, lp['mla_qn_g'])
    ckv = _rms(zmkv, lp['mla_kv_norm_g'])
    kr = _rope(zmkr[:, :, None, :], pos)[:, :, 0]
    return q, ckv, kr


def _diff_project(zd, lp):
    b, l, _ = zd.shape
    w = HEADS * 2 * DIFF_D
    q = _rms(zd[..., :w].reshape(b, l, HEADS, 2, DIFF_D), lp['dif_qn_g'])
    k = _rms(zd[..., w:2 * w].reshape(b, l, HEADS, 2, DIFF_D), lp['dif_kn_g'])
    return (q.reshape(b, l, HEADS, 2 * DIFF_D), k.reshape(b, l, HEADS, 2 * DIFF_D),
            zd[..., 2 * w:].reshape(b, l, HEADS, HEAD_W))


def _diff_lambda(lam, l):
    lam_init = 0.8 - 0.6 * math.exp(-0.3 * l)
    return jnp.exp(jnp.sum(lam[0] * lam[1])) - jnp.exp(jnp.sum(lam[2] * lam[3])) + lam_init, lam_init


def _prompt_attention(q_m, ckv, kr, q_d, k_d, v_d, lam, lp, rel_table, *, t):
    b, s = q_m.shape[:2]
    flat = ckv.reshape(b * s, MLA_KV_LORA)
    k_nope = _matmul(flat, lp['mla_w_uk'].reshape(MLA_KV_LORA, BRANCH_W), tm=2048).reshape(b, s, HEADS, MLA_NOPE)
    v_m = _matmul(flat, lp['mla_w_uv'].reshape(MLA_KV_LORA, BRANCH_W), tm=2048).reshape(b, s, HEADS, HEAD_W)
    k_rope = jnp.broadcast_to(kr[:, :, None, :], (b, s, HEADS, MLA_ROPE))
    k_m = _rms(jnp.concatenate([k_nope, k_rope], axis=-1), lp['mla_kn_g'])
    hm = lambda x: jnp.transpose(x, (0, 2, 1, 3)).astype(BF16)
    idx = jnp.arange(t)
    causal = jnp.where(idx[None, :] <= idx[:, None], 0.0, NEG).astype(F32)
    zeros = jnp.zeros((t, t), F32)
    o_mla = _flash(hm(q_m * MLA_QK ** -0.5), hm(k_m), hm(v_m), jnp.stack([causal, zeros, zeros])[None], t=t)
    o_mla = jnp.transpose(o_mla, (0, 2, 1, 3)).reshape(b, s, BRANCH_W)
    maps = lambda x: jnp.transpose(x.reshape(b, s, HEADS, 2, DIFF_D), (0, 2, 3, 1, 4)).reshape(b, 2 * HEADS, s, DIFF_D)
    bias = jnp.stack([_rel_bias(rel_table, idx, idx) + causal[None],
                      _rel_bias(rel_table, t + idx, idx),
                      _rel_bias(rel_table, 2 * t + idx, idx)], axis=1)
    o = _flash(maps(q_d * DIFF_D ** -0.5).astype(BF16), maps(k_d).astype(BF16), hm(v_d), bias, t=t)
    o = o.reshape(b, HEADS, 2, s, HEAD_W)
    o_dif = jnp.transpose(o[:, :, 0] - lam * o[:, :, 1], (0, 2, 1, 3))
    return o_mla, o_dif


def _sample_attention(q_m, ckv, kr, q_d, k_d, v_d, lam, lp, rel_table, page_table, caches, layer):
    b, l = q_m.shape[:2]
    ckv_pool, kr_pool, dk_pool, dv_pool = caches
    n_past = page_table.shape[1] * PAGE_SIZE
    pos = n_past + jnp.arange(l)
    tok = jnp.arange(SUBLANES)
    new_ok = (tok[None, :] <= tok[:, None]) & (tok[None, :] < l)
    mask_new = jnp.where(new_ok, 0.0, NEG).astype(F32)
    qg = q_m * lp['mla_kn_g'] * MLA_QK ** -0.5
    w_uk = lp['mla_w_uk']
    eye = jnp.eye(HEADS, dtype=F32)
    w_bd = jnp.einsum('rhd,hg->hdgr', w_uk, eye).reshape(BRANCH_W, HEADS * MLA_KV_LORA)
    qa = _matmul(qg[..., :MLA_NOPE].reshape(b * l, BRANCH_W), w_bd, precision=HIGHEST)
    qa = jnp.transpose(qa.reshape(b, l, HEADS, MLA_KV_LORA), (0, 2, 1, 3))
    qr = jnp.transpose(qg[..., MLA_NOPE:], (0, 2, 1, 3))
    rows = HEADS * SUBLANES
    qa = _pad_rows(qa, 2, SUBLANES).reshape(b, rows, MLA_KV_LORA).astype(BF16)
    qr = _pad_rows(qr, 2, SUBLANES).reshape(b, rows, MLA_ROPE).astype(BF16)
    wt = jnp.transpose(w_uk.reshape(MLA_KV_LORA, BRANCH_W)).astype(BF16)
    wuv = jnp.transpose(lp['mla_w_uv'], (1, 0, 2)).astype(BF16)
    o = _mla_decode(page_table, qa, qr, wt, wuv, _pad_rows(ckv, 1, SUBLANES),
                    jnp.transpose(_pad_rows(kr, 1, SUBLANES), (0, 2, 1)),
                    jnp.tile(mask_new, (HEADS, 1)), ckv_pool, kr_pool, layer=layer)
    o_mla = jnp.transpose(o.reshape(b, HEADS, SUBLANES, HEAD_W)[:, :, :l], (0, 2, 1, 3)).reshape(b, l, BRANCH_W)
    q5 = jnp.transpose(q_d.reshape(b, l, HEADS, 2, DIFF_D), (0, 2, 3, 1, 4)) * DIFF_D ** -0.5
    q5 = _pad_rows(q5, 3, SUBLANES)
    eye2 = jnp.eye(2, dtype=F32)
    q_bd = jnp.einsum('bhmtd,hg,mn->bhmtgnd', q5, eye, eye2).reshape(b, 2 * rows, BRANCH_W).astype(BF16)
    pc = 16
    n_last = pc * PAGE_SIZE
    expand = lambda x: jnp.broadcast_to(_pad_rows(x, 1, SUBLANES)[:, None], (HEADS, 2, SUBLANES, x.shape[-1])
                                        ).reshape(2 * rows, x.shape[-1])
    bias_last = expand(_rel_bias(rel_table, pos, jnp.arange(n_past - n_last, n_past)))
    bias_far = expand(_rel_bias(rel_table, pos, jnp.arange(1)))
    bias_new = _rel_bias(rel_table, n_past + tok, n_past + tok) + mask_new[None]
    bias_new = jnp.broadcast_to(bias_new[:, None], (HEADS, 2, SUBLANES, SUBLANES)).reshape(2 * rows, SUBLANES)
    flat_t = lambda x: jnp.transpose(_pad_rows(x.reshape(b, l, BRANCH_W), 1, SUBLANES), (0, 2, 1))
    o = _dif_decode(page_table, q_bd, flat_t(k_d), flat_t(v_d), bias_far, bias_last, bias_new, dk_pool, dv_pool,
                    layer=layer, pc=pc)
    o = o.reshape(b, HEADS, 2, SUBLANES, HEADS, HEAD_W)
    o = jnp.stack([o[:, h, :, :, h] for h in range(HEADS)], axis=1)
    o_dif = jnp.transpose(o[:, :, 0, :l] - lam * o[:, :, 1, :l], (0, 2, 1, 3))
    return o_mla, o_dif


def _group_branches(z, row0, b, l, pos, layer, lp, rel_table, st, paged):
    cols = lambda lo, width: z[row0:row0 + b * l, lo:lo + width].reshape(b, l, width)
    zrw = cols(Z_RW, RW_SHIFT_W)
    zmq, zmkv = cols(Z_MQ, MLA_Q_LORA), cols(Z_MKV, MLA_KV_LORA)
    zqkv, zz, zdif = cols(Z_QKV, GDN_QKV_W), cols(Z_ZZ, BRANCH_W), cols(Z_DIF, DIFF_QKV_W)
    zmkr = cols(Z_SMALL, MLA_ROPE)
    zb = cols(Z_SMALL + MLA_ROPE, HEADS)
    za = cols(Z_SMALL + MLA_ROPE + HEADS, HEADS)
    prompt = paged is None
    o_rw, rw_s, rw_shift = _rwkv_branch(zrw, st['rwkv_shift'], st['rwkv'], lp, bb=2)
    o_gdn, gdn_s, gdn_conv = _gdn_branch(zqkv, zb, za, zz, st['gdn_conv'], st['gdn'], lp, bb=2)
    q_m, ckv, kr = _mla_project(zmq, zmkv, zmkr, pos, lp)
    q_d, k_d, v_d = _diff_project(zdif, lp)
    lam, lam_init = _diff_lambda(lp['dif_lam'], layer)
    if prompt:
        o_mla, o_dif = _prompt_attention(q_m, ckv, kr, q_d, k_d, v_d, lam, lp, rel_table,
                                         t=min(FLASH_TILE, l))
    else:
        page_table, caches = paged
        o_mla, o_dif = _sample_attention(q_m, ckv, kr, q_d, k_d, v_d, lam, lp, rel_table, page_table, caches, layer)
    o_dif = (_rms(o_dif, lp['dif_subln_g']) * (1.0 - lam_init)).reshape(b, l, BRANCH_W)
    br = jnp.concatenate([o_rw, o_mla, o_gdn, o_dif], axis=-1).reshape(b * l, N_BRANCH * BRANCH_W)
    new = dict(mla_ckv=ckv, mla_krope=kr, diff_k=k_d, diff_v=v_d,
               rwkv=rw_s, rwkv_shift=rw_shift, gdn=gdn_s, gdn_conv=gdn_conv)
    return br, new


STATE_KEYS = ('mla_ckv', 'mla_krope', 'diff_k', 'diff_v', 'rwkv', 'rwkv_shift', 'gdn', 'gdn_conv')


def kernel(x_prompt, x_sample, cache_mla_ckv, cache_mla_krope, cache_diff_k, cache_diff_v, state_rwkv, state_rwkv_shift, state_gdn, state_gdn_conv, page_table, rel_bias, norm1_g, norm2_g, w_in, rw_mu, rw_w0, rw_w_up, rw_a0, rw_a_up, rw_g_up, rw_k_k, rw_k_a, rw_r_k, rw_ln_g, rw_ln_b, mla_q_norm_g, mla_w_uq, mla_kv_norm_g, mla_w_uk, mla_w_uv, mla_qn_g, mla_kn_g, gdn_conv_w, gdn_a_log, gdn_dt_bias, gdn_norm_g, dif_qn_g, dif_kn_g, dif_lam, dif_subln_g, w_branch, w_out, ffn_w_gate, ffn_w_up, ffn_w_down, moe_router, moe_w_gate, moe_w_up, moe_w_down):
    layer_params = dict(
        rw_mu=rw_mu, rw_w0=rw_w0, rw_w_up=rw_w_up, rw_a0=rw_a0, rw_a_up=rw_a_up, rw_g_up=rw_g_up,
        rw_k_k=rw_k_k, rw_k_a=rw_k_a, rw_r_k=rw_r_k, rw_ln_g=rw_ln_g, rw_ln_b=rw_ln_b,
        mla_q_norm_g=mla_q_norm_g, mla_w_uq=mla_w_uq, mla_kv_norm_g=mla_kv_norm_g, mla_w_uk=mla_w_uk,
        mla_w_uv=mla_w_uv, mla_qn_g=mla_qn_g, mla_kn_g=mla_kn_g, gdn_conv_w=gdn_conv_w, gdn_a_log=gdn_a_log,
        gdn_dt_bias=gdn_dt_bias, gdn_norm_g=gdn_norm_g, dif_qn_g=dif_qn_g, dif_kn_g=dif_kn_g, dif_lam=dif_lam,
        dif_subln_g=dif_subln_g)
    bp, sp, _ = x_prompt.shape
    bs, ls, _ = x_sample.shape
    tp, ts = bp * sp, bs * ls
    depth = w_in.shape[0]
    x = jnp.concatenate([x_prompt.reshape(tp, D_MODEL), x_sample.reshape(ts, D_MODEL)], axis=0)
    pos_p = jnp.arange(sp)
    pos_s = page_table.shape[1] * PAGE_SIZE + jnp.arange(ls)
    feat_major = lambda c: jnp.swapaxes(c.reshape(c.shape[:3] + (-1,)), 2, 3)
    caches = (cache_mla_ckv, feat_major(cache_mla_krope), feat_major(cache_diff_k), feat_major(cache_diff_v))
    sample_state = dict(rwkv=state_rwkv, rwkv_shift=state_rwkv_shift, gdn=state_gdn, gdn_conv=state_gdn_conv)
    got_p = {name: [] for name in STATE_KEYS}
    got_s = {name: [] for name in STATE_KEYS}
    for l in range(depth):
        lp = {name: arr[l] for name, arr in layer_params.items()}
        z = _rms_matmul(x, norm1_g[l], _pack_w_in(w_in[l]), tm=528, tn=2432, name="in_proj")
        st_p = dict(rwkv=jnp.zeros((bp, HEADS, HEAD_W, HEAD_W), F32), rwkv_shift=jnp.zeros((bp, RW_SHIFT_W), F32),
                    gdn=jnp.zeros((bp, HEADS, HEAD_W, HEAD_W), F32),
                    gdn_conv=jnp.zeros((bp, GDN_CONV - 1, GDN_QKV_W), F32))
        st_s = {name: arr[l] for name, arr in sample_state.items()}
        br_p, new_p = _group_branches(z, 0, bp, sp, pos_p, l, lp, rel_bias, st_p, None)
        br_s, new_s = _group_branches(z, tp, bs, ls, pos_s, l, lp, rel_bias, st_s, (page_table, caches))
        x = _merge_out(x, z, jnp.concatenate([br_p, br_s], axis=0), w_branch[l].astype(BF16), w_out[l].astype(BF16))
        if l % 2 == 0:
            combine = jnp.ones((1, tp + ts, 1), F32)
            x = _ffn(x, norm2_g[l], combine, ffn_w_gate[l // 2][None].astype(BF16), ffn_w_up[l // 2][None].astype(BF16),
                     ffn_w_down[l // 2][None].astype(BF16), tm=1056, tf=1408)
        else:
            router = jnp.pad(moe_router[l // 2], ((0, 0), (0, LANES - N_EXPERTS)))
            logits = _rms_matmul(x, norm2_g[l], router, tm=528, precision=HIGHEST, name="router")[:, :N_EXPERTS]
            top_v, top_i = lax.top_k(logits, TOP_K)
            top_w = jax.nn.softmax(top_v, axis=-1)
            combine = jnp.sum(jax.nn.one_hot(top_i, N_EXPERTS, dtype=F32) * top_w[..., None], axis=-2)
            combine = jnp.transpose(combine)[:, :, None]
            x = _ffn(x, norm2_g[l], combine, moe_w_gate[l // 2].astype(BF16), moe_w_up[l // 2].astype(BF16),
                     moe_w_down[l // 2].astype(BF16), tm=1056, tf=896)
        for name in STATE_KEYS:
            got_p[name].append(new_p[name])
            got_s[name].append(new_s[name])
    sp_out = {name: jnp.stack(v, axis=0) for name, v in got_p.items()}
    ss_out = {name: jnp.stack(v, axis=0) for name, v in got_s.items()}
    return ((x[:tp].reshape(bp, sp, D_MODEL), x[tp:].reshape(bs, ls, D_MODEL))
            + tuple(sp_out[name] for name in STATE_KEYS) + tuple(ss_out[name] for name in STATE_KEYS))
```
